```python
import jax, jax.numpy as jnp
from jax import lax
import numpy as np

D_MODEL = 4096
BATCH = 2
SEQ = 4096
DEPTH = 2

PLE_DIM = 256
POOL_GROUPS = 4
POOL_WIDTH = D_MODEL // 4
POOL_GROUP_DIM = POOL_WIDTH // POOL_GROUPS
POOL_WINDOWS = (2, 4, 8, 16)
GMLP_HEADS = 12
GMLP_HEAD_DIM = 128
GMLP_WIDTH = GMLP_HEADS * GMLP_HEAD_DIM
CHUNK = 128
CONV_HEADS = 12
CONV_HEAD_DIM = 128
CONV_WIDTH = CONV_HEADS * CONV_HEAD_DIM
CONV_K = 3
MIX_WIDTH = POOL_WIDTH + GMLP_WIDTH + CONV_WIDTH
IN_PROJ_WIDTH = POOL_WIDTH + 2 * GMLP_WIDTH + 3 * CONV_WIDTH
N_EXPERTS = 64
TOP_K = 8
N_GROUPS = 8
TOPK_GROUPS = 4
EXPERT_DIM = 384
SHARED_DIM = 384
ROUTED_SCALE = 2.5
ALPHA = (2 * DEPTH) ** 0.25
BETA = (8 * DEPTH) ** -0.25
LN_EPS = 1e-5
RMS_EPS = 1e-6

kernel_name = "hybrid_pool_sgu_conv_moe_deepnorm"


def layer_norm(x, g, b):
    xf = x.astype(jnp.float32)
    mu = jnp.mean(xf, axis=-1, keepdims=True)
    var = jnp.mean(jnp.square(xf - mu), axis=-1, keepdims=True)
    y = (xf - mu) * lax.rsqrt(var + LN_EPS) * g.astype(jnp.float32) + b.astype(jnp.float32)
    return y.astype(x.dtype)


def rms_norm(x, g):
    xf = x.astype(jnp.float32)
    y = xf * lax.rsqrt(jnp.mean(jnp.square(xf), axis=-1, keepdims=True) + RMS_EPS) * g.astype(jnp.float32)
    return y.astype(x.dtype)


def pool_mixer(a, w_pool):
    B, S, _ = a.shape
    a4 = a.reshape(B, S, POOL_GROUPS, POOL_GROUP_DIM).astype(jnp.float32)
    csum = jnp.pad(jnp.cumsum(a4, axis=1), ((0, 0), (1, 0), (0, 0), (0, 0)))
    steps = jnp.arange(1, S + 1)
    outs = []
    for g, w in enumerate(POOL_WINDOWS):
        cg = csum[:, :, g]
        lag = jnp.pad(cg[:, :S + 1 - w], ((0, 0), (w - 1, 0), (0, 0)))
        count = jnp.minimum(steps, w).astype(jnp.float32)[None, :, None]
        outs.append((cg[:, 1:] - lag) / count - a4[:, :, g])
    d = jnp.stack(outs, axis=2).astype(a.dtype)
    y = jnp.einsum('bsgi,gio->bsgo', d, w_pool)
    return y.reshape(B, S, POOL_WIDTH)


def gmlp_mixer(uv, ln_g, ln_b, w_spatial, b_spatial):
    u, v = jnp.split(jax.nn.gelu(uv), 2, axis=-1)
    v = layer_norm(v, ln_g, ln_b)
    B, S, _ = v.shape
    vc = v.reshape(B, S // CHUNK, CHUNK, GMLP_HEADS, GMLP_HEAD_DIM)
    mask = jnp.tril(jnp.ones((CHUNK, CHUNK), dtype=bool))
    ws = jnp.where(mask[None], w_spatial, 0)
    sv = jnp.einsum('hij,bcjhd->bcihd', ws, vc) + b_spatial.T[None, None, :, :, None]
    return u * sv.reshape(B, S, GMLP_WIDTH)


def conv_mixer(bch, w_conv):
    b_gate, c_gate, hh = jnp.split(bch, 3, axis=-1)
    z = c_gate * hh
    zc = lax.conv_general_dilated(
        z, w_conv[:, None, :], window_strides=(1,), padding=((CONV_K - 1, 0),),
        dimension_numbers=('NWC', 'WIO', 'NWC'), feature_group_count=CONV_WIDTH)
    return b_gate * zc


def swiglu(t, wg, wu, wd):
    return (jax.nn.silu(t @ wg) * (t @ wu)) @ wd


def moe(h, w_router, router_bias, w_exp_gate, w_exp_up, w_exp_down, w_sh_gate, w_sh_up, w_sh_down):
    B, S, D = h.shape
    t = h.reshape(B * S, D)
    scores = jax.nn.sigmoid((t @ w_router).astype(jnp.float32))
    sel = scores + router_bias.astype(jnp.float32)
    grp = sel.reshape(-1, N_GROUPS, N_EXPERTS // N_GROUPS)
    grp_score = jnp.sum(lax.top_k(grp, 2)[0], axis=-1)
    _, gidx = lax.top_k(grp_score, TOPK_GROUPS)
    gmask = jnp.sum(jax.nn.one_hot(gidx, N_GROUPS, dtype=jnp.float32), axis=1)
    emask = jnp.repeat(gmask, N_EXPERTS // N_GROUPS, axis=-1) > 0
    _, eidx = lax.top_k(jnp.where(emask, sel, -jnp.inf), TOP_K)
    w = jnp.take_along_axis(scores, eidx, axis=-1)
    w = ROUTED_SCALE * w / jnp.sum(w, axis=-1, keepdims=True)
    gates = jnp.einsum('tk,tke->te', w, jax.nn.one_hot(eidx, N_EXPERTS, dtype=jnp.float32)).astype(h.dtype)
    y = swiglu(t, w_sh_gate, w_sh_up, w_sh_down)
    for e in range(N_EXPERTS):
        y = y + gates[:, e:e + 1] * swiglu(t, w_exp_gate[e], w_exp_up[e], w_exp_down[e])
    return y.reshape(B, S, D)


def setup_inputs(seed: int = 0) -> dict:
    key = jax.random.key(seed)
    ks = jax.random.split(key, 30)
    f32 = jnp.float32
    L, D = DEPTH, D_MODEL

    def nrm(k, shape, scale):
        return jax.random.normal(k, shape, f32) * scale

    def gain(k, shape):
        return 1.0 + 0.01 * jax.random.normal(k, shape, f32)

    return {
        "x": jax.random.normal(ks[0], (BATCH, SEQ, D), f32),
        "p": jax.random.normal(ks[1], (DEPTH, BATCH, SEQ, PLE_DIM), f32),
        "w_in": nrm(ks[2], (L, D, IN_PROJ_WIDTH), D ** -0.5),
        "w_pool": nrm(ks[3], (L, POOL_GROUPS, POOL_GROUP_DIM, POOL_GROUP_DIM), POOL_GROUP_DIM ** -0.5),
        "pool_scale": gain(ks[4], (L, POOL_WIDTH)),
        "gmlp_ln_g": gain(ks[5], (L, GMLP_WIDTH)),
        "gmlp_ln_b": nrm(ks[6], (L, GMLP_WIDTH), 0.01),
        "w_spatial": nrm(ks[7], (L, GMLP_HEADS, CHUNK, CHUNK), 0.5 * CHUNK ** -0.5),
        "b_spatial": gain(ks[8], (L, GMLP_HEADS, CHUNK)),
        "w_conv": nrm(ks[9], (L, CONV_K, CONV_WIDTH), CONV_K ** -0.5),
        "norm_g_gmlp": gain(ks[10], (L, GMLP_WIDTH)),
        "norm_g_conv": gain(ks[11], (L, CONV_WIDTH)),
        "w_out": nrm(ks[12], (L, MIX_WIDTH, D), BETA * MIX_WIDTH ** -0.5),
        "ln1_g": gain(ks[13], (L, D)),
        "ln1_b": nrm(ks[14], (L, D), 0.01),
        "w_router": nrm(ks[15], (L, D, N_EXPERTS), D ** -0.5),
        "router_bias": nrm(ks[16], (L, N_EXPERTS), 0.01),
        "w_exp_gate": nrm(ks[17], (L, N_EXPERTS, D, EXPERT_DIM), D ** -0.5),
        "w_exp_up": nrm(ks[18], (L, N_EXPERTS, D, EXPERT_DIM), D ** -0.5),
        "w_exp_down": nrm(ks[19], (L, N_EXPERTS, EXPERT_DIM, D), BETA * EXPERT_DIM ** -0.5),
        "w_sh_gate": nrm(ks[20], (L, D, SHARED_DIM), D ** -0.5),
        "w_sh_up": nrm(ks[21], (L, D, SHARED_DIM), D ** -0.5),
        "w_sh_down": nrm(ks[22], (L, SHARED_DIM, D), BETA * SHARED_DIM ** -0.5),
        "ln2_g": gain(ks[23], (L, D)),
        "ln2_b": nrm(ks[24], (L, D), 0.01),
        "w_ple_gate": nrm(ks[25], (L, D, D), D ** -0.5),
        "b_ple_gate": nrm(ks[26], (L, D), 0.01),
        "w_ple_proj": nrm(ks[27], (L, PLE_DIM, D), BETA * PLE_DIM ** -0.5),
        "ln3_g": gain(ks[28], (L, D)),
        "ln3_b": nrm(ks[29], (L, D), 0.01),
    }


def reference(x, p, w_in, w_pool, pool_scale, gmlp_ln_g, gmlp_ln_b, w_spatial, b_spatial, w_conv,
              norm_g_gmlp, norm_g_conv, w_out, ln1_g, ln1_b, w_router, router_bias,
              w_exp_gate, w_exp_up, w_exp_down, w_sh_gate, w_sh_up, w_sh_down, ln2_g, ln2_b,
              w_ple_gate, b_ple_gate, w_ple_proj, ln3_g, ln3_b):
    h = x
    for i in range(DEPTH):
        z = h @ w_in[i]
        a = z[..., :POOL_WIDTH]
        uv = z[..., POOL_WIDTH:POOL_WIDTH + 2 * GMLP_WIDTH]
        bch = z[..., POOL_WIDTH + 2 * GMLP_WIDTH:]
        y_pool = rms_norm(pool_mixer(a, w_pool[i]), pool_scale[i])
        y_gmlp = rms_norm(gmlp_mixer(uv, gmlp_ln_g[i], gmlp_ln_b[i], w_spatial[i], b_spatial[i]), norm_g_gmlp[i])
        y_conv = rms_norm(conv_mixer(bch, w_conv[i]), norm_g_conv[i])
        mix = jnp.concatenate([y_pool, y_gmlp, y_conv], axis=-1) @ w_out[i]
        h = layer_norm(ALPHA * h + mix, ln1_g[i], ln1_b[i])
        ffn = moe(h, w_router[i], router_bias[i], w_exp_gate[i], w_exp_up[i], w_exp_down[i],
                  w_sh_gate[i], w_sh_up[i], w_sh_down[i])
        h = layer_norm(ALPHA * h + ffn, ln2_g[i], ln2_b[i])
        ple = jax.nn.sigmoid(h @ w_ple_gate[i] + b_ple_gate[i]) * (p[i] @ w_ple_proj[i])
        h = layer_norm(ALPHA * h + ple, ln3_g[i], ln3_b[i])
    return h
```

```python
import functools
import math

import jax
import jax.numpy as jnp
from jax import lax
from jax.experimental import pallas as pl
from jax.experimental.pallas import tpu as pltpu

F32 = jnp.float32
BF16 = jnp.bfloat16

POOL_WINDOWS = (2, 4, 8, 16)
TOP_K = 8
N_GROUPS = 8
TOPK_GROUPS = 4
ROUTED_SCALE = 2.5
LN_EPS = 1e-5
RMS_EPS = 1e-6

V7X_LANES = 128
V7X_SUBLANES = 8
V7X_VMEM_LIMIT_BYTES = 60 * 1024 * 1024
MAX_HALO = 16


def _tile(n, pref, align):
    t = min(pref, n)
    t -= t % align
    while t >= align:
        if n % t == 0:
            return t
        t -= align
    return n


def _params(sem):
    return pltpu.CompilerParams(dimension_semantics=sem, vmem_limit_bytes=V7X_VMEM_LIMIT_BYTES)


def _sigmoid(x):
    return 1.0 / (1.0 + jnp.exp(-x))


def _matmul_kernel(x_ref, w_ref, o_ref):
    o_ref[...] = jnp.dot(x_ref[...], w_ref[...], preferred_element_type=F32).astype(o_ref.dtype)


def matmul(x, w, out_dtype):
    m, k = x.shape
    n = w.shape[1]
    tm = _tile(m, 1024, V7X_SUBLANES * 2)
    tn = _tile(n, 512, V7X_LANES)
    return pl.pallas_call(
        _matmul_kernel,
        grid=(m // tm, n // tn),
        in_specs=[pl.BlockSpec((tm, k), lambda i, j: (i, 0)),
                  pl.BlockSpec((k, tn), lambda i, j: (0, j))],
        out_specs=pl.BlockSpec((tm, tn), lambda i, j: (i, j)),
        out_shape=jax.ShapeDtypeStruct((m, n), out_dtype),
        compiler_params=_params(("parallel", "arbitrary")),
        name="in_proj",
    )(x, w)


def _mixer_kernel(z_ref, wpool_ref, pscale_ref, lng_ref, lnb_ref, ws_ref, bs_ref, wconv_ref,
                  ngg_ref, ngc_ref, o_ref, pool_carry, conv_carry, gm_buf, *, dims):
    pool_w, gdim, gm_w, chunk, conv_w = dims
    ts = z_ref.shape[0]
    s = pl.program_id(1)
    n_groups = pool_w // gdim
    heads = gm_w // chunk

    @pl.when(s == 0)
    def _():
        pool_carry[...] = jnp.zeros_like(pool_carry)
        conv_carry[...] = jnp.zeros_like(conv_carry)

    pos = (s * ts + lax.broadcasted_iota(jnp.int32, (ts, 1), 0)).astype(F32)

    a = z_ref[:, 0:pool_w].astype(F32)
    ext = jnp.concatenate([pool_carry[...], a], axis=0)
    pool_carry[...] = a[ts - MAX_HALO:, :]
    ys = []
    ssq = jnp.zeros((ts, 1), F32)
    for g in range(n_groups):
        win = POOL_WINDOWS[g]
        acc = ext[:, g * gdim:(g + 1) * gdim]
        span = 1
        while span < win:
            acc = acc + pltpu.roll(acc, span, axis=0)
            span *= 2
        count = jnp.minimum(pos + 1.0, float(win))
        d = acc[MAX_HALO:, :] / count - a[:, g * gdim:(g + 1) * gdim]
        y = jnp.dot(d.astype(BF16), wpool_ref[g], preferred_element_type=F32)
        ssq = ssq + jnp.sum(y * y, axis=-1, keepdims=True)
        ys.append(y)
    rinv = lax.rsqrt(ssq / pool_w + RMS_EPS)
    for g in range(n_groups):
        o_ref[:, g * gdim:(g + 1) * gdim] = (
            ys[g] * rinv * pscale_ref[:, g * gdim:(g + 1) * gdim]).astype(o_ref.dtype)

    uv = jax.nn.gelu(z_ref[:, pool_w:pool_w + 2 * gm_w].astype(F32))
    u = uv[:, :gm_w]
    v = uv[:, gm_w:]
    mu = jnp.mean(v, axis=-1, keepdims=True)
    vc = v - mu
    var = jnp.mean(vc * vc, axis=-1, keepdims=True)
    vn = (vc * lax.rsqrt(var + LN_EPS) * lng_ref[...] + lnb_ref[...]).astype(BF16)
    row = lax.broadcasted_iota(jnp.int32, (chunk, chunk), 0)
    col = lax.broadcasted_iota(jnp.int32, (chunk, chunk), 1)
    ssq = jnp.zeros((ts, 1), F32)
    for h in range(heads):
        wsm = jnp.where(row >= col, ws_ref[h], 0.0).astype(BF16)
        bias = bs_ref[:, h:h + 1]
        parts = []
        for c in range(ts // chunk):
            blk = vn[c * chunk:(c + 1) * chunk, h * chunk:(h + 1) * chunk]
            parts.append(jnp.dot(wsm, blk, preferred_element_type=F32) + bias)
        sv = jnp.concatenate(parts, axis=0) if len(parts) > 1 else parts[0]
        y = u[:, h * chunk:(h + 1) * chunk] * sv
        ssq = ssq + jnp.sum(y * y, axis=-1, keepdims=True)
        gm_buf[:, h * chunk:(h + 1) * chunk] = y
    rinv = lax.rsqrt(ssq / gm_w + RMS_EPS)
    o_ref[:, pool_w:pool_w + gm_w] = (gm_buf[...] * rinv * ngg_ref[...]).astype(o_ref.dtype)

    off = pool_w + 2 * gm_w
    bg = z_ref[:, off:off + conv_w].astype(F32)
    x = z_ref[:, off + conv_w:off + 2 * conv_w].astype(F32) * z_ref[:, off + 2 * conv_w:off + 3 * conv_w].astype(F32)
    ext = jnp.concatenate([conv_carry[...], x], axis=0)
    conv_carry[...] = x[ts - V7X_SUBLANES:, :]
    zc = (wconv_ref[0:1, :] * pltpu.roll(ext, 2, axis=0)[V7X_SUBLANES:, :]
          + wconv_ref[1:2, :] * pltpu.roll(ext, 1, axis=0)[V7X_SUBLANES:, :]
          + wconv_ref[2:3, :] * x)
    y = bg * zc
    rinv = lax.rsqrt(jnp.mean(y * y, axis=-1, keepdims=True) + RMS_EPS)
    o_ref[:, pool_w + gm_w:pool_w + gm_w + conv_w] = (y * rinv * ngc_ref[...]).astype(o_ref.dtype)


def mixer(z, batch, seq, w_pool, pool_scale, ln_g, ln_b, w_spatial, b_spatial_t, w_conv, ng_gmlp, ng_conv):
    n_groups, gdim, _ = w_pool.shape
    pool_w = n_groups * gdim
    heads, chunk, _ = w_spatial.shape
    gm_w = heads * chunk
    conv_w = w_conv.shape[1]
    assert w_conv.shape[0] == 3 and n_groups == len(POOL_WINDOWS)
    width = z.shape[1]
    assert width == pool_w + 2 * gm_w + 3 * conv_w
    mix_w = pool_w + gm_w + conv_w
    ts = _tile(seq, 256, chunk)
    n_s = seq // ts
    full = lambda a: pl.BlockSpec(a.shape, lambda b, s: (0,) * a.ndim)
    kern = functools.partial(_mixer_kernel, dims=(pool_w, gdim, gm_w, chunk, conv_w))
    return pl.pallas_call(
        kern,
        grid=(batch, n_s),
        in_specs=[pl.BlockSpec((ts, width), lambda b, s: (b * n_s + s, 0)),
                  full(w_pool), full(pool_scale), full(ln_g), full(ln_b), full(w_spatial),
                  full(b_spatial_t), full(w_conv), full(ng_gmlp), full(ng_conv)],
        out_specs=pl.BlockSpec((ts, mix_w), lambda b, s: (b * n_s + s, 0)),
        out_shape=jax.ShapeDtypeStruct((batch * seq, mix_w), BF16),
        scratch_shapes=[pltpu.VMEM((MAX_HALO, pool_w), F32),
                        pltpu.VMEM((V7X_SUBLANES, conv_w), F32),
                        pltpu.VMEM((ts, gm_w), F32)],
        compiler_params=_params(("arbitrary", "arbitrary")),
        name="mixer",
    )(z, w_pool, pool_scale, ln_g, ln_b, w_spatial, b_spatial_t, w_conv, ng_gmlp, ng_conv)


def _layer_norm_store(acc, n_j, tn, n, g_ref, b_ref, o32_ref, o16_ref):
    tot = jnp.sum(acc[0], axis=-1, keepdims=True)
    for jj in range(1, n_j):
        tot = tot + jnp.sum(acc[jj], axis=-1, keepdims=True)
    mu = tot / n
    c0 = acc[0] - mu
    sq = jnp.sum(c0 * c0, axis=-1, keepdims=True)
    for jj in range(1, n_j):
        c = acc[jj] - mu
        sq = sq + jnp.sum(c * c, axis=-1, keepdims=True)
    rstd = lax.rsqrt(sq / n + LN_EPS)
    for jj in range(n_j):
        sl = slice(jj * tn, (jj + 1) * tn)
        y = (acc[jj] - mu) * rstd * g_ref[:, sl] + b_ref[:, sl]
        o32_ref[:, sl] = y
        o16_ref[:, sl] = y.astype(BF16)


def _proj_ln_kernel(x_ref, w_ref, res_ref, g_ref, b_ref, o32_ref, o16_ref, acc, *, alpha):
    j = pl.program_id(1)
    n_j, _, tn = acc.shape
    y = jnp.dot(x_ref[...], w_ref[...], preferred_element_type=F32)
    acc[j] = alpha * res_ref[...] + y

    @pl.when(j == n_j - 1)
    def _():
        _layer_norm_store(acc, n_j, tn, n_j * tn, g_ref, b_ref, o32_ref, o16_ref)


def _ple_ln_kernel(x_ref, w_ref, bias_ref, p_ref, wp_ref, res_ref, g_ref, b_ref, o32_ref, o16_ref, acc,
                   *, alpha):
    j = pl.program_id(1)
    n_j, _, tn = acc.shape
    gate = _sigmoid(jnp.dot(x_ref[...], w_ref[...], preferred_element_type=F32) + bias_ref[...])
    proj = jnp.dot(p_ref[...].astype(BF16), wp_ref[...], preferred_element_type=F32)
    acc[j] = alpha * res_ref[...] + gate * proj

    @pl.when(j == n_j - 1)
    def _():
        _layer_norm_store(acc, n_j, tn, n_j * tn, g_ref, b_ref, o32_ref, o16_ref)


def proj_ln(x, w, res, ln_g, ln_b, alpha, ple=None):
    m, k = x.shape
    n = w.shape[1]
    tm = _tile(m, 512, V7X_SUBLANES * 2)
    tn = _tile(n, 512, V7X_LANES)
    n_j = n // tn
    x_spec = pl.BlockSpec((tm, k), lambda i, j: (i, 0))
    w_spec = pl.BlockSpec((k, tn), lambda i, j: (0, j))
    col_spec = pl.BlockSpec((tm, tn), lambda i, j: (i, j))
    vec_full = pl.BlockSpec((1, n), lambda i, j: (0, 0))
    row_spec = pl.BlockSpec((tm, n), lambda i, j: (i, 0), pipeline_mode=pl.Buffered(1))
    if ple is None:
        kern = functools.partial(_proj_ln_kernel, alpha=alpha)
        in_specs = [x_spec, w_spec, col_spec, vec_full, vec_full]
        args = (x, w, res, ln_g, ln_b)
        name = "out_proj_ln"
    else:
        bias, p, w_p = ple
        kp = p.shape[1]
        kern = functools.partial(_ple_ln_kernel, alpha=alpha)
        in_specs = [x_spec, w_spec, pl.BlockSpec((1, tn), lambda i, j: (0, j)),
                    pl.BlockSpec((tm, kp), lambda i, j: (i, 0)),
                    pl.BlockSpec((kp, tn), lambda i, j: (0, j)),
                    col_spec, vec_full, vec_full]
        args = (x, w, bias, p, w_p, res, ln_g, ln_b)
        name = "ple_ln"
    return pl.pallas_call(
        kern,
        grid=(m // tm, n_j),
        in_specs=in_specs,
        out_specs=[row_spec, row_spec],
        out_shape=[jax.ShapeDtypeStruct((m, n), F32), jax.ShapeDtypeStruct((m, n), BF16)],
        scratch_shapes=[pltpu.VMEM((n_j, tm, tn), F32)],
        compiler_params=_params(("parallel", "arbitrary")),
        name=name,
    )(*args)


def _first_argmax(x, idx, axis, big):
    m = jnp.max(x, axis=axis, keepdims=True)
    first = jnp.min(jnp.where(x == m, idx, big), axis=axis, keepdims=True)
    return m, first


def _router_kernel(x_ref, wh_ref, wl_ref, bias_ref, eidx_ref, wts_ref, rank_ref, cnt_ref, carry):
    tm = x_ref.shape[0]
    n_e = wh_ref.shape[0]
    per = n_e // N_GROUPS
    i = pl.program_id(0)

    @pl.when(i == 0)
    def _():
        carry[...] = jnp.zeros_like(carry)

    x = x_ref[...]
    xh = x.astype(BF16)
    xl = (x - xh.astype(F32)).astype(BF16)
    nt = (((1,), (1,)), ((), ()))
    logits = (lax.dot_general(wh_ref[...], xh, nt, preferred_element_type=F32)
              + lax.dot_general(wh_ref[...], xl, nt, preferred_element_type=F32)
              + lax.dot_general(wl_ref[...], xh, nt, preferred_element_type=F32))
    scores = _sigmoid(logits)
    sel = scores + bias_ref[...]

    sel3 = sel.reshape(N_GROUPS, per, tm)
    j_idx = lax.broadcasted_iota(jnp.int32, (N_GROUPS, per, tm), 1).astype(F32)
    m1, a1 = _first_argmax(sel3, j_idx, 1, float(per))
    m2 = jnp.max(jnp.where(j_idx == a1, -jnp.inf, sel3), axis=1, keepdims=True)
    gscore = (m1 + m2).reshape(N_GROUPS, tm)
    g_idx = lax.broadcasted_iota(jnp.int32, (N_GROUPS, tm), 0).astype(F32)
    gmask = jnp.zeros((N_GROUPS, tm), F32)
    work = gscore
    for _ in range(TOPK_GROUPS):
        _, ga = _first_argmax(work, g_idx, 0, float(N_GROUPS))
        hit = g_idx == ga
        gmask = jnp.where(hit, 1.0, gmask)
        work = jnp.where(hit, -jnp.inf, work)
    e_idx = lax.broadcasted_iota(jnp.int32, (n_e, tm), 0).astype(F32)
    emask = jnp.broadcast_to(gmask.reshape(N_GROUPS, 1, tm), (N_GROUPS, per, tm)).reshape(n_e, tm)
    work = jnp.where(emask > 0.5, sel, -jnp.inf)

    e_rows, s_rows, hots = [], [], []
    for _ in range(TOP_K):
        _, ea = _first_argmax(work, e_idx, 0, float(n_e))
        hit = e_idx == ea
        work = jnp.where(hit, -jnp.inf, work)
        e_rows.append(ea)
        s_rows.append(jnp.sum(jnp.where(hit, scores, 0.0), axis=0, keepdims=True))
        hots.append(jnp.where(hit, 1.0, 0.0))
    e_sel = jnp.concatenate(e_rows, axis=0)
    s_sel = jnp.concatenate(s_rows, axis=0)
    eidx_ref[...] = e_sel.astype(jnp.int32)
    wts_ref[...] = ROUTED_SCALE * s_sel / jnp.sum(s_sel, axis=0, keepdims=True)

    hot = jnp.concatenate(hots, axis=0)
    r_i = lax.broadcasted_iota(jnp.int32, (tm, tm), 0)
    c_i = lax.broadcasted_iota(jnp.int32, (tm, tm), 1)
    upper = jnp.where(r_i < c_i, 1.0, 0.0).astype(BF16)
    prefix = jnp.dot(hot.astype(BF16), upper, preferred_element_type=F32)
    base = carry[...]
    ranks = []
    for k in range(TOP_K):
        hk = hots[k]
        pk = prefix[k * n_e:(k + 1) * n_e, :] + base
        ranks.append(jnp.sum(hk * pk, axis=0, keepdims=True))
        base = base + jnp.sum(hk, axis=1, keepdims=True)
    rank_ref[...] = jnp.concatenate(ranks, axis=0).astype(jnp.int32)
    carry[...] = base
    cnt_ref[...] = base.astype(jnp.int32)


def router(h, w_hi_t, w_lo_t, bias_col):
    t, d = h.shape
    n_e = w_hi_t.shape[0]
    tm = _tile(t, 256, V7X_LANES)
    full = lambda a: pl.BlockSpec(a.shape, lambda i: (0,) * a.ndim)
    tok_spec = pl.BlockSpec((TOP_K, tm), lambda i: (0, i))
    return pl.pallas_call(
        _router_kernel,
        grid=(t // tm,),
        in_specs=[pl.BlockSpec((tm, d), lambda i: (i, 0)), full(w_hi_t), full(w_lo_t), full(bias_col)],
        out_specs=[tok_spec, tok_spec, tok_spec, pl.BlockSpec((n_e, 1), lambda i: (0, 0))],
        out_shape=[jax.ShapeDtypeStruct((TOP_K, t), jnp.int32),
                   jax.ShapeDtypeStruct((TOP_K, t), F32),
                   jax.ShapeDtypeStruct((TOP_K, t), jnp.int32),
                   jax.ShapeDtypeStruct((n_e, 1), jnp.int32)],
        scratch_shapes=[pltpu.VMEM((n_e, 1), F32)],
        compiler_params=_params(("arbitrary",)),
        name="router",
    )(h, w_hi_t, w_lo_t, bias_col)


def _row_gather(idx_ref, n_rows, src_hbm, dst, sem):
    def body(r, carry):
        pltpu.make_async_copy(src_hbm.at[pl.ds(idx_ref[0, 0, r], 1)], dst.at[pl.ds(r, 1)], sem).start()
        return carry
    lax.fori_loop(0, n_rows, body, 0)


def _moe_kernel(te_ref, nu_ref, tok_ref, tok_next_ref, x_hbm, wg_ref, wu_ref, wd_ref, y_ref, xbuf, sems):
    del te_ref
    i = pl.program_id(0)
    n_used = nu_ref[0]
    tm = xbuf.shape[1]
    slot = i % 2

    @pl.when(i == 0)
    def _():
        _row_gather(tok_ref, tm, x_hbm, xbuf.at[0], sems.at[0])

    @pl.when(i + 1 < n_used)
    def _():
        _row_gather(tok_next_ref, tm, x_hbm, xbuf.at[1 - slot], sems.at[1 - slot])

    @pl.when(i < n_used)
    def _():
        pltpu.make_async_copy(x_hbm.at[pl.ds(0, tm)], xbuf.at[slot], sems.at[slot]).wait()
        xb = xbuf[slot].astype(BF16)
        g = jnp.dot(xb, wg_ref[...], preferred_element_type=F32)
        u = jnp.dot(xb, wu_ref[...], preferred_element_type=F32)
        a = (g * _sigmoid(g) * u).astype(BF16)
        y_ref[...] = jnp.dot(a, wd_ref[...], preferred_element_type=F32)

    @pl.when(i >= n_used)
    def _():
        y_ref[...] = jnp.zeros_like(y_ref)


def moe_grouped(x, tok_sorted, tile_expert, n_used, wg, wu, wd, tm):
    n_tiles = tile_expert.shape[0]
    d = x.shape[1]
    f = wg.shape[2]
    tok3 = tok_sorted.reshape(n_tiles, 1, tm)
    smem_blk = lambda fn: pl.BlockSpec((1, 1, tm), fn, memory_space=pltpu.SMEM)
    grid_spec = pltpu.PrefetchScalarGridSpec(
        num_scalar_prefetch=2,
        grid=(n_tiles,),
        in_specs=[smem_blk(lambda i, te, nu: (i, 0, 0)),
                  smem_blk(lambda i, te, nu: (jnp.minimum(i + 1, n_tiles - 1), 0, 0)),
                  pl.BlockSpec(memory_space=pl.ANY),
                  pl.BlockSpec((None, d, f), lambda i, te, nu: (te[i], 0, 0)),
                  pl.BlockSpec((None, d, f), lambda i, te, nu: (te[i], 0, 0)),
                  pl.BlockSpec((None, f, d), lambda i, te, nu: (te[i], 0, 0))],
        out_specs=pl.BlockSpec((tm, d), lambda i, te, nu: (i, 0)),
        scratch_shapes=[pltpu.VMEM((2, tm, d), F32), pltpu.SemaphoreType.DMA((2,))],
    )
    return pl.pallas_call(
        _moe_kernel,
        grid_spec=grid_spec,
        out_shape=jax.ShapeDtypeStruct((n_tiles * tm, d), F32),
        compiler_params=_params(("arbitrary",)),
        name="moe_grouped",
    )(tile_expert, n_used, tok3, tok3, x, wg, wu, wd)


def _combine_gather(pos_ref, tm, ys_hbm, dst, sem):
    def body(t, carry):
        for k in range(TOP_K):
            pltpu.make_async_copy(ys_hbm.at[pl.ds(pos_ref[0, 0, t * TOP_K + k], 1)],
                                  dst.at[k, pl.ds(t, 1)], sem).start()
        return carry
    lax.fori_loop(0, tm, body, 0)


def _combine_kernel(pos_ref, pos_next_ref, wts_ref, h_ref, ysh_ref, ys_hbm, g_ref, b_ref,
                    o32_ref, o16_ref, gbuf, sems, *, alpha):
    i = pl.program_id(0)
    n_i = pl.num_programs(0)
    tm, d = h_ref.shape
    slot = i % 2

    @pl.when(i == 0)
    def _():
        _combine_gather(pos_ref, tm, ys_hbm, gbuf.at[0], sems.at[0])

    @pl.when(i + 1 < n_i)
    def _():
        _combine_gather(pos_next_ref, tm, ys_hbm, gbuf.at[1 - slot], sems.at[1 - slot])

    for k in range(TOP_K):
        pltpu.make_async_copy(ys_hbm.at[pl.ds(0, tm)], gbuf.at[slot, k], sems.at[slot]).wait()
    acc = alpha * h_ref[...] + ysh_ref[...]
    for k in range(TOP_K):
        acc = acc + wts_ref[:, k:k + 1] * gbuf[slot, k]
    mu = jnp.mean(acc, axis=-1, keepdims=True)
    c = acc - mu
    var = jnp.mean(c * c, axis=-1, keepdims=True)
    y = c * lax.rsqrt(var + LN_EPS) * g_ref[...] + b_ref[...]
    o32_ref[...] = y
    o16_ref[...] = y.astype(BF16)


def combine_ln(h, y_shared, y_sorted, pos, wts, ln_g, ln_b, alpha):
    t, d = h.shape
    tm = _tile(t, 64, V7X_SUBLANES * 2)
    n_i = t // tm
    pos3 = pos.reshape(n_i, 1, tm * TOP_K)
    smem_blk = lambda fn: pl.BlockSpec((1, 1, tm * TOP_K), fn, memory_space=pltpu.SMEM)
    row = pl.BlockSpec((tm, d), lambda i: (i, 0))
    vec = pl.BlockSpec((1, d), lambda i: (0, 0))
    return pl.pallas_call(
        functools.partial(_combine_kernel, alpha=alpha),
        grid=(n_i,),
        in_specs=[smem_blk(lambda i: (i, 0, 0)),
                  smem_blk(lambda i: (jnp.minimum(i + 1, n_i - 1), 0, 0)),
                  pl.BlockSpec((tm, TOP_K), lambda i: (i, 0)),
                  row, row, pl.BlockSpec(memory_space=pl.ANY), vec, vec],
        out_specs=[row, row],
        out_shape=[jax.ShapeDtypeStruct((t, d), F32), jax.ShapeDtypeStruct((t, d), BF16)],
        scratch_shapes=[pltpu.VMEM((2, TOP_K, tm, d), F32), pltpu.SemaphoreType.DMA((2,))],
        compiler_params=_params(("arbitrary",)),
        name="combine_ln",
    )(pos3, pos3, wts, h, y_shared, y_sorted, ln_g, ln_b)


def _dispatch_plan(eidx, rank, counts, tm, n_tiles):
    t = eidx.shape[0]
    n_e = counts.shape[0]
    padded = ((counts + tm - 1) // tm) * tm
    ends = jnp.cumsum(padded)
    starts = ends - padded
    pos = starts[eidx] + rank
    tok = jnp.arange(t * TOP_K, dtype=jnp.int32) // TOP_K
    tok_sorted = jnp.zeros((n_tiles * tm,), jnp.int32).at[pos.reshape(-1)].set(
        tok, unique_indices=True, mode="promise_in_bounds")
    tile_start = jnp.arange(n_tiles, dtype=jnp.int32) * tm
    tile_expert = jnp.minimum(jnp.searchsorted(ends, tile_start, side="right"), n_e - 1).astype(jnp.int32)
    n_used = (ends[-1] // tm).astype(jnp.int32).reshape(1)
    return pos.astype(jnp.int32), tok_sorted, tile_expert, n_used


def kernel(x, p, w_in, w_pool, pool_scale, gmlp_ln_g, gmlp_ln_b, w_spatial, b_spatial, w_conv, norm_g_gmlp,
           norm_g_conv, w_out, ln1_g, ln1_b, w_router, router_bias, w_exp_gate, w_exp_up, w_exp_down,
           w_sh_gate, w_sh_up, w_sh_down, ln2_g, ln2_b, w_ple_gate, b_ple_gate, w_ple_proj, ln3_g, ln3_b):
    depth = w_in.shape[0]
    batch, seq, d = x.shape
    t = batch * seq
    n_e = w_router.shape[2]
    alpha = (2.0 * depth) ** 0.25
    row = lambda a: a.reshape(1, -1)

    tm_e = _tile(t * TOP_K // n_e, 256, V7X_SUBLANES * 2)
    n_tiles = t * TOP_K // tm_e + n_e
    tm_s = _tile(t, 256, V7X_SUBLANES * 2)
    ident = jnp.arange(t, dtype=jnp.int32)
    shared_tiles = jnp.zeros((t // tm_s,), jnp.int32)
    shared_used = jnp.full((1,), t // tm_s, jnp.int32)

    h32 = x.reshape(t, d)
    h16 = h32.astype(BF16)
    p2 = p.reshape(depth, t, p.shape[-1])
    for i in range(depth):
        z = matmul(h16, w_in[i].astype(BF16), BF16)
        ycat = mixer(z, batch, seq, w_pool[i].astype(BF16), row(pool_scale[i]), row(gmlp_ln_g[i]),
                     row(gmlp_ln_b[i]), w_spatial[i], b_spatial[i].T, w_conv[i], row(norm_g_gmlp[i]),
                     row(norm_g_conv[i]))
        h32, h16 = proj_ln(ycat, w_out[i].astype(BF16), h32, row(ln1_g[i]), row(ln1_b[i]), alpha)
        wr_t = w_router[i].T
        wr_hi = wr_t.astype(BF16)
        wr_lo = (wr_t - wr_hi.astype(F32)).astype(BF16)
        eidx_t, wts_t, rank_t, counts = router(h32, wr_hi, wr_lo, router_bias[i].reshape(n_e, 1))
        pos, tok_sorted, tile_expert, n_used = _dispatch_plan(eidx_t.T, rank_t.T, counts[:, 0], tm_e, n_tiles)
        y_sorted = moe_grouped(h32, tok_sorted, tile_expert, n_used, w_exp_gate[i].astype(BF16),
                               w_exp_up[i].astype(BF16), w_exp_down[i].astype(BF16), tm_e)
        y_shared = moe_grouped(h32, ident, shared_tiles, shared_used, w_sh_gate[i].astype(BF16)[None],
                               w_sh_up[i].astype(BF16)[None], w_sh_down[i].astype(BF16)[None], tm_s)
        h32, h16 = combine_ln(h32, y_shared, y_sorted, pos, wts_t.T, row(ln2_g[i]), row(ln2_b[i]), alpha)
        h32, h16 = proj_ln(h16, w_ple_gate[i].astype(BF16), h32, row(ln3_g[i]), row(ln3_b[i]), alpha,
                           ple=(row(b_ple_gate[i]), p2[i], w_ple_proj[i].astype(BF16)))
    return h32.reshape(batch, seq, d)
```

```python
import functools
import math

import jax
import jax.numpy as jnp
from jax import lax
from jax.experimental import pallas as pl
from jax.experimental.pallas import tpu as pltpu

F32 = jnp.float32
BF16 = jnp.bfloat16

POOL_WINDOWS = (2, 4, 8, 16)
TOP_K = 8
N_GROUPS = 8
TOPK_GROUPS = 4
ROUTED_SCALE = 2.5
LN_EPS = 1e-5
RMS_EPS = 1e-6

V7X_LANES = 128
V7X_SUBLANES = 8
V7X_VMEM_LIMIT_BYTES = 60 * 1024 * 1024
MAX_HALO = 16
MOE_ROW_GROUPS = 2

def _tile(n, pref, align):
    t = min(pref, n)
    t -= t % align
    while t >= align:
        if n % t == 0:
            return t
        t -= align
    return n


def _params(sem, flags=None):
    return pltpu.CompilerParams(dimension_semantics=sem, vmem_limit_bytes=V7X_VMEM_LIMIT_BYTES, flags=flags)


def _sigmoid(x):
    return 1.0 / (1.0 + jnp.exp(-x))


def _matmul_kernel(x_ref, w_ref, o_ref):
    o_ref[...] = jnp.dot(x_ref[...], w_ref[...], preferred_element_type=F32).astype(o_ref.dtype)


def matmul(x, w, out_dtype):
    m, k = x.shape
    n = w.shape[1]
    tm = _tile(m, 1024, V7X_SUBLANES * 2)
    tn = _tile(n, 512, V7X_LANES)
    return pl.pallas_call(
        _matmul_kernel,
        grid=(m // tm, n // tn),
        in_specs=[pl.BlockSpec((tm, k), lambda i, j: (i, 0)),
                  pl.BlockSpec((k, tn), lambda i, j: (0, j))],
        out_specs=pl.BlockSpec((tm, tn), lambda i, j: (i, j)),
        out_shape=jax.ShapeDtypeStruct((m, n), out_dtype),
        compiler_params=_params(("parallel", "arbitrary")),
        name="in_proj",
    )(x, w)


def _mixer_kernel(z_ref, wpool_ref, pscale_ref, lng_ref, lnb_ref, ws_ref, bs_ref, wconv_ref,
                  ngg_ref, ngc_ref, o_ref, pool_carry, conv_carry, gm_buf, *, dims):
    pool_w, gdim, gm_w, chunk, conv_w = dims
    ts = z_ref.shape[0]
    s = pl.program_id(1)
    n_groups = pool_w // gdim
    heads = gm_w // chunk

    @pl.when(s == 0)
    def _():
        pool_carry[...] = jnp.zeros_like(pool_carry)
        conv_carry[...] = jnp.zeros_like(conv_carry)

    pos = (s * ts + lax.broadcasted_iota(jnp.int32, (ts, 1), 0)).astype(F32)

    a = z_ref[:, 0:pool_w].astype(F32)
    ext = jnp.concatenate([pool_carry[...], a], axis=0)
    pool_carry[...] = a[ts - MAX_HALO:, :]
    ys = []
    ssq = jnp.zeros((ts, 1), F32)
    for g in range(n_groups):
        win = POOL_WINDOWS[g]
        acc = ext[:, g * gdim:(g + 1) * gdim]
        span = 1
        while span < win:
            acc = acc + pltpu.roll(acc, span, axis=0)
            span *= 2
        count = jnp.minimum(pos + 1.0, float(win))
        d = acc[MAX_HALO:, :] / count - a[:, g * gdim:(g + 1) * gdim]
        y = jnp.dot(d.astype(BF16), wpool_ref[g], preferred_element_type=F32)
        ssq = ssq + jnp.sum(y * y, axis=-1, keepdims=True)
        ys.append(y)
    rinv = lax.rsqrt(ssq / pool_w + RMS_EPS)
    for g in range(n_groups):
        o_ref[:, g * gdim:(g + 1) * gdim] = (
            ys[g] * rinv * pscale_ref[:, g * gdim:(g + 1) * gdim]).astype(o_ref.dtype)

    uv = jax.nn.gelu(z_ref[:, pool_w:pool_w + 2 * gm_w].astype(F32))
    u = uv[:, :gm_w]
    v = uv[:, gm_w:]
    mu = jnp.mean(v, axis=-1, keepdims=True)
    vc = v - mu
    var = jnp.mean(vc * vc, axis=-1, keepdims=True)
    vn = (vc * lax.rsqrt(var + LN_EPS) * lng_ref[...] + lnb_ref[...]).astype(BF16)
    row = lax.broadcasted_iota(jnp.int32, (chunk, chunk), 0)
    col = lax.broadcasted_iota(jnp.int32, (chunk, chunk), 1)
    ssq = jnp.zeros((ts, 1), F32)
    for h in range(heads):
        wsm = jnp.where(row >= col, ws_ref[h], 0.0).astype(BF16)
        bias = bs_ref[:, h:h + 1]
        parts = []
        for c in range(ts // chunk):
            blk = vn[c * chunk:(c + 1) * chunk, h * chunk:(h + 1) * chunk]
            parts.append(jnp.dot(wsm, blk, preferred_element_type=F32) + bias)
        sv = jnp.concatenate(parts, axis=0) if len(parts) > 1 else parts[0]
        y = u[:, h * chunk:(h + 1) * chunk] * sv
        ssq = ssq + jnp.sum(y * y, axis=-1, keepdims=True)
        gm_buf[:, h * chunk:(h + 1) * chunk] = y
    rinv = lax.rsqrt(ssq / gm_w + RMS_EPS)
    o_ref[:, pool_w:pool_w + gm_w] = (gm_buf[...] * rinv * ngg_ref[...]).astype(o_ref.dtype)

    off = pool_w + 2 * gm_w
    bg = z_ref[:, off:off + conv_w].astype(F32)
    x = z_ref[:, off + conv_w:off + 2 * conv_w].astype(F32) * z_ref[:, off + 2 * conv_w:off + 3 * conv_w].astype(F32)
    ext = jnp.concatenate([conv_carry[...], x], axis=0)
    conv_carry[...] = x[ts - V7X_SUBLANES:, :]
    zc = (wconv_ref[0:1, :] * pltpu.roll(ext, 2, axis=0)[V7X_SUBLANES:, :]
          + wconv_ref[1:2, :] * pltpu.roll(ext, 1, axis=0)[V7X_SUBLANES:, :]
          + wconv_ref[2:3, :] * x)
    y = bg * zc
    rinv = lax.rsqrt(jnp.mean(y * y, axis=-1, keepdims=True) + RMS_EPS)
    o_ref[:, pool_w + gm_w:pool_w + gm_w + conv_w] = (y * rinv * ngc_ref[...]).astype(o_ref.dtype)


def mixer(z, batch, seq, w_pool, pool_scale, ln_g, ln_b, w_spatial, b_spatial_t, w_conv, ng_gmlp, ng_conv):
    n_groups, gdim, _ = w_pool.shape
    pool_w = n_groups * gdim
    heads, chunk, _ = w_spatial.shape
    gm_w = heads * chunk
    conv_w = w_conv.shape[1]
    assert w_conv.shape[0] == 3 and n_groups == len(POOL_WINDOWS)
    width = z.shape[1]
    assert width == pool_w + 2 * gm_w + 3 * conv_w
    mix_w = pool_w + gm_w + conv_w
    ts = _tile(seq, 256, chunk)
    n_s = seq // ts
    full = lambda a: pl.BlockSpec(a.shape, lambda b, s: (0,) * a.ndim)
    kern = functools.partial(_mixer_kernel, dims=(pool_w, gdim, gm_w, chunk, conv_w))
    return pl.pallas_call(
        kern,
        grid=(batch, n_s),
        in_specs=[pl.BlockSpec((ts, width), lambda b, s: (b * n_s + s, 0)),
                  full(w_pool), full(pool_scale), full(ln_g), full(ln_b), full(w_spatial),
                  full(b_spatial_t), full(w_conv), full(ng_gmlp), full(ng_conv)],
        out_specs=pl.BlockSpec((ts, mix_w), lambda b, s: (b * n_s + s, 0)),
        out_shape=jax.ShapeDtypeStruct((batch * seq, mix_w), BF16),
        scratch_shapes=[pltpu.VMEM((MAX_HALO, pool_w), F32),
                        pltpu.VMEM((V7X_SUBLANES, conv_w), F32),
                        pltpu.VMEM((ts, gm_w), F32)],
        compiler_params=_params(("arbitrary", "arbitrary")),
        name="mixer",
    )(z, w_pool, pool_scale, ln_g, ln_b, w_spatial, b_spatial_t, w_conv, ng_gmlp, ng_conv)


def _pack_pair(lo, hi):
    lo_bits = lax.bitcast_convert_type(lo.astype(BF16).astype(F32), jnp.uint32) >> 16
    hi_bits = lax.bitcast_convert_type(hi.astype(BF16).astype(F32), jnp.uint32) & jnp.uint32(0xFFFF0000)
    return lo_bits | hi_bits


def _unpack_lo(words):
    return lax.bitcast_convert_type(words << 16, F32)


def _unpack_hi(words):
    return lax.bitcast_convert_type(words & jnp.uint32(0xFFFF0000), F32)


def _layer_norm_store(acc, n_j, tn, n, g_ref, b_ref, o32_ref, o16_ref):
    tot = jnp.sum(acc[0], axis=-1, keepdims=True)
    for jj in range(1, n_j):
        tot = tot + jnp.sum(acc[jj], axis=-1, keepdims=True)
    mu = tot / n
    c0 = acc[0] - mu
    sq = jnp.sum(c0 * c0, axis=-1, keepdims=True)
    for jj in range(1, n_j):
        c = acc[jj] - mu
        sq = sq + jnp.sum(c * c, axis=-1, keepdims=True)
    rstd = lax.rsqrt(sq / n + LN_EPS)

    def norm(jj):
        sl = slice(jj * tn, (jj + 1) * tn)
        y = (acc[jj] - mu) * rstd * g_ref[:, sl] + b_ref[:, sl]
        o32_ref[:, sl] = y
        return y

    if o16_ref.dtype == jnp.uint32:
        for jj in range(n_j // 2):
            o16_ref[:, jj * tn:(jj + 1) * tn] = _pack_pair(norm(jj), norm(jj + n_j // 2))
    else:
        for jj in range(n_j):
            o16_ref[:, jj * tn:(jj + 1) * tn] = norm(jj).astype(BF16)


def _proj_ln_kernel(x_ref, w_ref, res_ref, g_ref, b_ref, o32_ref, o16_ref, acc, *, alpha):
    j = pl.program_id(1)
    n_j, _, tn = acc.shape
    y = jnp.dot(x_ref[...], w_ref[...], preferred_element_type=F32)
    acc[j] = alpha * res_ref[...] + y

    @pl.when(j == n_j - 1)
    def _():
        _layer_norm_store(acc, n_j, tn, n_j * tn, g_ref, b_ref, o32_ref, o16_ref)


def _ple_ln_kernel(x_ref, w_ref, bias_ref, p_ref, wp_ref, res_ref, g_ref, b_ref, o32_ref, o16_ref, acc,
                   *, alpha):
    j = pl.program_id(1)
    n_j, _, tn = acc.shape
    gate = _sigmoid(jnp.dot(x_ref[...], w_ref[...], preferred_element_type=F32) + bias_ref[...])
    proj = jnp.dot(p_ref[...].astype(BF16), wp_ref[...], preferred_element_type=F32)
    acc[j] = alpha * res_ref[...] + gate * proj

    @pl.when(j == n_j - 1)
    def _():
        _layer_norm_store(acc, n_j, tn, n_j * tn, g_ref, b_ref, o32_ref, o16_ref)


def proj_ln(x, w, res, ln_g, ln_b, alpha, ple=None, packed=False):
    m, k = x.shape
    n = w.shape[1]
    tm = _tile(m, 512, V7X_SUBLANES * 2)
    tn = _tile(n // 2 if packed else n, 512, V7X_LANES)
    n_j = n // tn
    x_spec = pl.BlockSpec((tm, k), lambda i, j: (i, 0))
    w_spec = pl.BlockSpec((k, tn), lambda i, j: (0, j))
    col_spec = pl.BlockSpec((tm, tn), lambda i, j: (i, j))
    vec_full = pl.BlockSpec((1, n), lambda i, j: (0, 0))
    row_spec = pl.BlockSpec((tm, n), lambda i, j: (i, 0), pipeline_mode=pl.Buffered(1))
    half_spec = pl.BlockSpec((tm, n // 2), lambda i, j: (i, 0), pipeline_mode=pl.Buffered(1))
    if ple is None:
        kern = functools.partial(_proj_ln_kernel, alpha=alpha)
        in_specs = [x_spec, w_spec, col_spec, vec_full, vec_full]
        args = (x, w, res, ln_g, ln_b)
        name = "out_proj_ln"
    else:
        bias, p, w_p = ple
        kp = p.shape[1]
        kern = functools.partial(_ple_ln_kernel, alpha=alpha)
        in_specs = [x_spec, w_spec, pl.BlockSpec((1, tn), lambda i, j: (0, j)),
                    pl.BlockSpec((tm, kp), lambda i, j: (i, 0)),
                    pl.BlockSpec((kp, tn), lambda i, j: (0, j)),
                    col_spec, vec_full, vec_full]
        args = (x, w, bias, p, w_p, res, ln_g, ln_b)
        name = "ple_ln"
    return pl.pallas_call(
        kern,
        grid=(m // tm, n_j),
        in_specs=in_specs,
        out_specs=[row_spec, half_spec if packed else row_spec],
        out_shape=[jax.ShapeDtypeStruct((m, n), F32),
                   jax.ShapeDtypeStruct((m, n // 2), jnp.uint32) if packed else jax.ShapeDtypeStruct((m, n), BF16)],
        scratch_shapes=[pltpu.VMEM((n_j, tm, tn), F32)],
        compiler_params=_params(("parallel", "arbitrary")),
        name=name,
    )(*args)


def _first_argmax(x, idx, axis, big):
    m = jnp.max(x, axis=axis, keepdims=True)
    first = jnp.min(jnp.where(x == m, idx, big), axis=axis, keepdims=True)
    return m, first


def _router_kernel(x_ref, wh_ref, wl_ref, bias_ref, eidx_ref, wts_ref, rank_ref, cnt_ref, carry):
    tm = x_ref.shape[0]
    n_e = wh_ref.shape[0]
    per = n_e // N_GROUPS
    i = pl.program_id(0)

    @pl.when(i == 0)
    def _():
        carry[...] = jnp.zeros_like(carry)

    x = x_ref[...]
    xh = x.astype(BF16)
    xl = (x - xh.astype(F32)).astype(BF16)
    nt = (((1,), (1,)), ((), ()))
    logits = (lax.dot_general(wh_ref[...], xh, nt, preferred_element_type=F32)
              + lax.dot_general(wh_ref[...], xl, nt, preferred_element_type=F32)
              + lax.dot_general(wl_ref[...], xh, nt, preferred_element_type=F32))
    scores = _sigmoid(logits)
    sel = scores + bias_ref[...]

    sel3 = sel.reshape(N_GROUPS, per, tm)
    j_idx = lax.broadcasted_iota(jnp.int32, (N_GROUPS, per, tm), 1).astype(F32)
    m1, a1 = _first_argmax(sel3, j_idx, 1, float(per))
    m2 = jnp.max(jnp.where(j_idx == a1, -jnp.inf, sel3), axis=1, keepdims=True)
    gscore = (m1 + m2).reshape(N_GROUPS, tm)
    g_idx = lax.broadcasted_iota(jnp.int32, (N_GROUPS, tm), 0).astype(F32)
    gmask = jnp.zeros((N_GROUPS, tm), F32)
    work = gscore
    for _ in range(TOPK_GROUPS):
        _, ga = _first_argmax(work, g_idx, 0, float(N_GROUPS))
        hit = g_idx == ga
        gmask = jnp.where(hit, 1.0, gmask)
        work = jnp.where(hit, -jnp.inf, work)
    e_idx = lax.broadcasted_iota(jnp.int32, (n_e, tm), 0).astype(F32)
    emask = jnp.broadcast_to(gmask.reshape(N_GROUPS, 1, tm), (N_GROUPS, per, tm)).reshape(n_e, tm)
    work = jnp.where(emask > 0.5, sel, -jnp.inf)

    e_rows, s_rows, hots = [], [], []
    for _ in range(TOP_K):
        _, ea = _first_argmax(work, e_idx, 0, float(n_e))
        hit = e_idx == ea
        work = jnp.where(hit, -jnp.inf, work)
        e_rows.append(ea)
        s_rows.append(jnp.sum(jnp.where(hit, scores, 0.0), axis=0, keepdims=True))
        hots.append(jnp.where(hit, 1.0, 0.0))
    e_sel = jnp.concatenate(e_rows, axis=0)
    s_sel = jnp.concatenate(s_rows, axis=0)
    eidx_ref[...] = e_sel.astype(jnp.int32)
    wts_ref[...] = ROUTED_SCALE * s_sel / jnp.sum(s_sel, axis=0, keepdims=True)

    hot = jnp.concatenate(hots, axis=0)
    r_i = lax.broadcasted_iota(jnp.int32, (tm, tm), 0)
    c_i = lax.broadcasted_iota(jnp.int32, (tm, tm), 1)
    upper = jnp.where(r_i < c_i, 1.0, 0.0).astype(BF16)
    prefix = jnp.dot(hot.astype(BF16), upper, preferred_element_type=F32)
    base = carry[...]
    ranks = []
    for k in range(TOP_K):
        hk = hots[k]
        pk = prefix[k * n_e:(k + 1) * n_e, :] + base
        ranks.append(jnp.sum(hk * pk, axis=0, keepdims=True))
        base = base + jnp.sum(hk, axis=1, keepdims=True)
    rank_ref[...] = jnp.concatenate(ranks, axis=0).astype(jnp.int32)
    carry[...] = base
    cnt_ref[...] = base.astype(jnp.int32)


def router(h, w_hi_t, w_lo_t, bias_col):
    t, d = h.shape
    n_e = w_hi_t.shape[0]
    tm = _tile(t, 256, V7X_LANES)
    full = lambda a: pl.BlockSpec(a.shape, lambda i: (0,) * a.ndim)
    tok_spec = pl.BlockSpec((TOP_K, tm), lambda i: (0, i))
    return pl.pallas_call(
        _router_kernel,
        grid=(t // tm,),
        in_specs=[pl.BlockSpec((tm, d), lambda i: (i, 0)), full(w_hi_t), full(w_lo_t), full(bias_col)],
        out_specs=[tok_spec, tok_spec, tok_spec, pl.BlockSpec((n_e, 1), lambda i: (0, 0))],
        out_shape=[jax.ShapeDtypeStruct((TOP_K, t), jnp.int32),
                   jax.ShapeDtypeStruct((TOP_K, t), F32),
                   jax.ShapeDtypeStruct((TOP_K, t), jnp.int32),
                   jax.ShapeDtypeStruct((n_e, 1), jnp.int32)],
        scratch_shapes=[pltpu.VMEM((n_e, 1), F32)],
        compiler_params=_params(("arbitrary",)),
        name="router",
    )(h, w_hi_t, w_lo_t, bias_col)


def _moe_kernel(te_ref, nu_ref, tok_cur, tok_next, dst_prev, x_hbm, wgu_ref, wd_ref, y_hbm,
                x_in, x_bf, y_out, y_new, gsem, ssem):
    i = pl.program_id(0)
    n_used = nu_ref[0]
    tm, half = x_in.shape
    f = wd_ref.shape[0]

    def gather_row(idx_ref, r):
        pltpu.make_async_copy(x_hbm.at[pl.ds(idx_ref[0, 0, r], 1)], x_in.at[pl.ds(r, 1)], gsem).start()

    def scatter_row(r):
        pltpu.make_async_copy(y_out.at[pl.ds(r, 1)], y_hbm.at[pl.ds(dst_prev[0, 0, r], 1)], ssem).start()

    def wait_gather():
        pltpu.make_async_copy(x_hbm.at[pl.ds(0, tm)], x_in, gsem).wait()

    def wait_scatter():
        pltpu.make_async_copy(y_out, y_hbm.at[pl.ds(0, tm)], ssem).wait()

    def loop_rows(fn):
        def body(r, carry):
            fn(r)
            return carry
        lax.fori_loop(0, tm, body, 0)

    @pl.when(i == 0)
    def _():
        y_new[...] = jnp.zeros_like(y_new)
        loop_rows(lambda r: gather_row(tok_cur, r))

    @pl.when(jnp.logical_and(i >= 1, i <= n_used))
    def _():
        wait_scatter()

    @pl.when(i < n_used)
    def _():
        wait_gather()
        words = x_in[...]
        x_bf[:, 0:half] = _unpack_lo(words).astype(BF16)
        x_bf[:, half:2 * half] = _unpack_hi(words).astype(BF16)
        y_out[...] = y_new[...]
        for r in range(tm):
            gather_row(tok_next, r)
            scatter_row(r)

    @pl.when(jnp.logical_and(i < n_used, te_ref[i] >= 0))
    def _():
        rows = tm // MOE_ROW_GROUPS
        for h in range(MOE_ROW_GROUPS):
            rs = slice(h * rows, (h + 1) * rows)
            gu = jnp.dot(x_bf[rs, :], wgu_ref[...], preferred_element_type=F32)
            g = gu[:, 0:f]
            a = (g * _sigmoid(g) * gu[:, f:2 * f]).astype(BF16)
            y = jnp.dot(a, wd_ref[...], preferred_element_type=F32)
            y_new[rs, :] = _pack_pair(y[:, 0:half], y[:, half:2 * half])

    @pl.when(i == n_used)
    def _():
        wait_gather()
        y_out[...] = y_new[...]
        loop_rows(scatter_row)
        wait_scatter()


def moe_grouped(x_packed, dst_all, tile_expert, n_used, wgu, wd, layer, tm, out_rows):
    n_grid = tile_expert.shape[0]
    t, half = x_packed.shape
    f = wd.shape[2]
    dst3 = dst_all.reshape(n_grid + 1, 1, tm)
    tok3 = dst3 % t
    smem_blk = lambda fn: pl.BlockSpec((1, 1, tm), fn, memory_space=pltpu.SMEM)
    grid_spec = pltpu.PrefetchScalarGridSpec(
        num_scalar_prefetch=2,
        grid=(n_grid,),
        in_specs=[smem_blk(lambda i, te, nu: (i + 1, 0, 0)),
                  smem_blk(lambda i, te, nu: (jnp.minimum(i + 2, n_grid), 0, 0)),
                  smem_blk(lambda i, te, nu: (i, 0, 0)),
                  pl.BlockSpec(memory_space=pl.ANY),
                  pl.BlockSpec((None, None, 2 * half, 2 * f), lambda i, te, nu: (layer, te[i], 0, 0)),
                  pl.BlockSpec((None, None, f, 2 * half), lambda i, te, nu: (layer, te[i], 0, 0))],
        out_specs=pl.BlockSpec(memory_space=pl.ANY),
        scratch_shapes=[pltpu.VMEM((tm, half), jnp.uint32), pltpu.VMEM((tm, 2 * half), BF16),
                        pltpu.VMEM((tm, half), jnp.uint32), pltpu.VMEM((tm, half), jnp.uint32),
                        pltpu.SemaphoreType.DMA(()), pltpu.SemaphoreType.DMA(())],
    )
    return pl.pallas_call(
        _moe_kernel,
        grid_spec=grid_spec,
        out_shape=jax.ShapeDtypeStruct((out_rows, half), jnp.uint32),
        compiler_params=_params(("arbitrary",)),
        name="moe_grouped",
    )(tile_expert, n_used, tok3, tok3, dst3, x_packed, wgu, wd)


def _combine_kernel(wts_ref, h_ref, sh_ref, *rest, alpha):
    planes = rest[:TOP_K]
    g_ref, b_ref, o32_ref, o16_ref = rest[TOP_K:]
    d = h_ref.shape[1]
    half = d // 2
    sh = sh_ref[...]
    lo = alpha * h_ref[:, 0:half] + _unpack_lo(sh)
    hi = alpha * h_ref[:, half:d] + _unpack_hi(sh)
    for k in range(TOP_K):
        w = wts_ref[:, k:k + 1]
        words = planes[k][...]
        lo = lo + w * _unpack_lo(words)
        hi = hi + w * _unpack_hi(words)
    mu = (jnp.sum(lo, axis=-1, keepdims=True) + jnp.sum(hi, axis=-1, keepdims=True)) / d
    lo = lo - mu
    hi = hi - mu
    var = (jnp.sum(lo * lo, axis=-1, keepdims=True) + jnp.sum(hi * hi, axis=-1, keepdims=True)) / d
    rstd = lax.rsqrt(var + LN_EPS)
    y_lo = lo * rstd * g_ref[:, 0:half] + b_ref[:, 0:half]
    y_hi = hi * rstd * g_ref[:, half:d] + b_ref[:, half:d]
    o32_ref[:, 0:half] = y_lo
    o32_ref[:, half:d] = y_hi
    o16_ref[:, 0:half] = y_lo.astype(BF16)
    o16_ref[:, half:d] = y_hi.astype(BF16)


def combine_ln(h, y_shared, y_slots, wts, ln_g, ln_b, alpha):
    t, d = h.shape
    half = d // 2
    tm = _tile(t, 128, V7X_SUBLANES * 2)
    n_i = t // tm
    row = pl.BlockSpec((tm, d), lambda i: (i, 0))
    vec = pl.BlockSpec((1, d), lambda i: (0, 0))
    plane = lambda k: pl.BlockSpec((tm, half), lambda i: (k * n_i + i, 0))
    return pl.pallas_call(
        functools.partial(_combine_kernel, alpha=alpha),
        grid=(n_i,),
        in_specs=[pl.BlockSpec((tm, TOP_K), lambda i: (i, 0)), row, plane(0)]
        + [plane(k) for k in range(TOP_K)] + [vec, vec],
        out_specs=[row, row],
        out_shape=[jax.ShapeDtypeStruct((t, d), F32), jax.ShapeDtypeStruct((t, d), BF16)],
        compiler_params=_params(("parallel",)),
        name="combine_ln",
    )(wts, h, y_shared, *([y_slots] * TOP_K), ln_g, ln_b)


def _dispatch_plan(eidx_t, rank_t, counts, tm, n_grid):
    t = eidx_t.shape[1]
    n_e = counts.shape[0]
    padded = ((counts + tm - 1) // tm) * tm
    ends = jnp.cumsum(padded)
    starts = ends - padded
    pos = starts[eidx_t] + rank_t
    slot = (jnp.arange(TOP_K, dtype=jnp.int32)[:, None] * t + jnp.arange(t, dtype=jnp.int32)[None, :])
    spare = TOP_K * t + jnp.arange((n_grid + 1) * tm, dtype=jnp.int32) % tm
    dst_all = spare.at[(tm + pos).reshape(-1)].set(slot.reshape(-1), unique_indices=True,
                                                   mode="promise_in_bounds")
    tile_start = jnp.arange(n_grid, dtype=jnp.int32) * tm
    tile_expert = jnp.minimum(jnp.sum(ends[None, :] <= tile_start[:, None], axis=1), n_e - 1).astype(jnp.int32)
    n_used = (ends[-1] // tm).astype(jnp.int32).reshape(1)
    return dst_all, tile_expert, n_used


def kernel(x, p, w_in, w_pool, pool_scale, gmlp_ln_g, gmlp_ln_b, w_spatial, b_spatial, w_conv, norm_g_gmlp,
           norm_g_conv, w_out, ln1_g, ln1_b, w_router, router_bias, w_exp_gate, w_exp_up, w_exp_down,
           w_sh_gate, w_sh_up, w_sh_down, ln2_g, ln2_b, w_ple_gate, b_ple_gate, w_ple_proj, ln3_g, ln3_b):
    depth = w_in.shape[0]
    batch, seq, d = x.shape
    t = batch * seq
    n_e = w_router.shape[2]
    alpha = (2.0 * depth) ** 0.25
    row = lambda a: a.reshape(1, -1)

    tm_e = _tile(t * TOP_K // n_e, 256, V7X_SUBLANES * 2)
    n_grid = t * TOP_K // tm_e + n_e
    tm_s = _tile(t, 256, V7X_SUBLANES * 2)
    n_grid_s = t // tm_s + 1
    spare_s = t + jnp.arange(tm_s, dtype=jnp.int32)
    dst_shared = jnp.concatenate([spare_s, jnp.arange(t, dtype=jnp.int32), spare_s])
    shared_tiles = jnp.zeros((n_grid_s,), jnp.int32)
    shared_used = jnp.full((1,), t // tm_s, jnp.int32)

    wgu16 = jnp.concatenate([w_exp_gate, w_exp_up], axis=-1).astype(BF16)
    wd16 = w_exp_down.astype(BF16)
    sgu16 = jnp.concatenate([w_sh_gate, w_sh_up], axis=-1).astype(BF16)[:, None]
    sd16 = w_sh_down.astype(BF16)[:, None]

    h32 = x.reshape(t, d)
    h16 = h32.astype(BF16)
    p2 = p.reshape(depth, t, p.shape[-1])
    for i in range(depth):
        z = matmul(h16, w_in[i].astype(BF16), BF16)
        ycat = mixer(z, batch, seq, w_pool[i].astype(BF16), row(pool_scale[i]), row(gmlp_ln_g[i]),
                     row(gmlp_ln_b[i]), w_spatial[i], b_spatial[i].T, w_conv[i], row(norm_g_gmlp[i]),
                     row(norm_g_conv[i]))
        h32, hpk = proj_ln(ycat, w_out[i].astype(BF16), h32, row(ln1_g[i]), row(ln1_b[i]), alpha, packed=True)
        wr_t = w_router[i].T
        wr_hi = wr_t.astype(BF16)
        wr_lo = (wr_t - wr_hi.astype(F32)).astype(BF16)
        eidx_t, wts_t, rank_t, counts = router(h32, wr_hi, wr_lo, router_bias[i].reshape(n_e, 1))
        dst_all, tile_expert, n_used = _dispatch_plan(eidx_t, rank_t, counts[:, 0], tm_e, n_grid)
        y_slots = moe_grouped(hpk, dst_all, tile_expert, n_used, wgu16, wd16, i, tm_e, TOP_K * t + tm_e)
        y_shared = moe_grouped(hpk, dst_shared, shared_tiles, shared_used, sgu16, sd16, i, tm_s, t + tm_s)
        h32, h16 = combine_ln(h32, y_shared, y_slots, wts_t.T, row(ln2_g[i]), row(ln2_b[i]), alpha)
        h32, h16 = proj_ln(h16, w_ple_gate[i].astype(BF16), h32, row(ln3_g[i]), row(ln3_b[i]), alpha,
                           ple=(row(b_ple_gate[i]), p2[i], w_ple_proj[i].astype(BF16)))
    return h32.reshape(batch, seq, d)
```

```python
import functools
import math

import jax
import jax.numpy as jnp
from jax import lax
from jax.experimental import pallas as pl
from jax.experimental.pallas import tpu as pltpu

F32 = jnp.float32
BF16 = jnp.bfloat16

POOL_WINDOWS = (2, 4, 8, 16)
TOP_K = 8
N_GROUPS = 8
TOPK_GROUPS = 4
ROUTED_SCALE = 2.5
LN_EPS = 1e-5
RMS_EPS = 1e-6

V7X_LANES = 128
V7X_SUBLANES = 8
V7X_VMEM_LIMIT_BYTES = 60 * 1024 * 1024
MAX_HALO = 16
MOE_ROW_GROUPS = 2
def _tile(n, pref, align):
    t = min(pref, n)
    t -= t % align
    while t >= align:
        if n % t == 0:
            return t
        t -= align
    return n


def _params(sem, flags=None):
    return pltpu.CompilerParams(dimension_semantics=sem, vmem_limit_bytes=V7X_VMEM_LIMIT_BYTES, flags=flags)


def _sigmoid(x):
    return 1.0 / (1.0 + jnp.exp(-x))


def _matmul_kernel(x_ref, w_ref, o_ref):
    o_ref[...] = jnp.dot(x_ref[...], w_ref[...], preferred_element_type=F32).astype(o_ref.dtype)


def matmul(x, w, out_dtype):
    m, k = x.shape
    n = w.shape[1]
    tm = _tile(m, 1024, V7X_SUBLANES * 2)
    tn = _tile(n, 512, V7X_LANES)
    return pl.pallas_call(
        _matmul_kernel,
        grid=(m // tm, n // tn),
        in_specs=[pl.BlockSpec((tm, k), lambda i, j: (i, 0)),
                  pl.BlockSpec((k, tn), lambda i, j: (0, j))],
        out_specs=pl.BlockSpec((tm, tn), lambda i, j: (i, j)),
        out_shape=jax.ShapeDtypeStruct((m, n), out_dtype),
        compiler_params=_params(("parallel", "arbitrary")),
        name="in_proj",
    )(x, w)


def _mixer_kernel(z_ref, wpool_ref, pscale_ref, lng_ref, lnb_ref, ws_ref, bs_ref, wconv_ref,
                  ngg_ref, ngc_ref, o_ref, pool_carry, conv_carry, gm_buf, *, dims):
    pool_w, gdim, gm_w, chunk, conv_w = dims
    ts = z_ref.shape[0]
    s = pl.program_id(1)
    n_groups = pool_w // gdim
    heads = gm_w // chunk

    @pl.when(s == 0)
    def _():
        pool_carry[...] = jnp.zeros_like(pool_carry)
        conv_carry[...] = jnp.zeros_like(conv_carry)

    pos = (s * ts + lax.broadcasted_iota(jnp.int32, (ts, 1), 0)).astype(F32)

    a = z_ref[:, 0:pool_w].astype(F32)
    ext = jnp.concatenate([pool_carry[...], a], axis=0)
    pool_carry[...] = a[ts - MAX_HALO:, :]
    ys = []
    ssq = jnp.zeros((ts, 1), F32)
    for g in range(n_groups):
        win = POOL_WINDOWS[g]
        acc = ext[:, g * gdim:(g + 1) * gdim]
        span = 1
        while span < win:
            acc = acc + pltpu.roll(acc, span, axis=0)
            span *= 2
        count = jnp.minimum(pos + 1.0, float(win))
        d = acc[MAX_HALO:, :] / count - a[:, g * gdim:(g + 1) * gdim]
        y = jnp.dot(d.astype(BF16), wpool_ref[g], preferred_element_type=F32)
        ssq = ssq + jnp.sum(y * y, axis=-1, keepdims=True)
        ys.append(y)
    rinv = lax.rsqrt(ssq / pool_w + RMS_EPS)
    for g in range(n_groups):
        o_ref[:, g * gdim:(g + 1) * gdim] = (
            ys[g] * rinv * pscale_ref[:, g * gdim:(g + 1) * gdim]).astype(o_ref.dtype)

    uv = jax.nn.gelu(z_ref[:, pool_w:pool_w + 2 * gm_w].astype(F32))
    u = uv[:, :gm_w]
    v = uv[:, gm_w:]
    mu = jnp.mean(v, axis=-1, keepdims=True)
    vc = v - mu
    var = jnp.mean(vc * vc, axis=-1, keepdims=True)
    vn = (vc * lax.rsqrt(var + LN_EPS) * lng_ref[...] + lnb_ref[...]).astype(BF16)
    row = lax.broadcasted_iota(jnp.int32, (chunk, chunk), 0)
    col = lax.broadcasted_iota(jnp.int32, (chunk, chunk), 1)
    ssq = jnp.zeros((ts, 1), F32)
    for h in range(heads):
        wsm = jnp.where(row >= col, ws_ref[h], 0.0).astype(BF16)
        bias = bs_ref[:, h:h + 1]
        parts = []
        for c in range(ts // chunk):
            blk = vn[c * chunk:(c + 1) * chunk, h * chunk:(h + 1) * chunk]
            parts.append(jnp.dot(wsm, blk, preferred_element_type=F32) + bias)
        sv = jnp.concatenate(parts, axis=0) if len(parts) > 1 else parts[0]
        y = u[:, h * chunk:(h + 1) * chunk] * sv
        ssq = ssq + jnp.sum(y * y, axis=-1, keepdims=True)
        gm_buf[:, h * chunk:(h + 1) * chunk] = y
    rinv = lax.rsqrt(ssq / gm_w + RMS_EPS)
    o_ref[:, pool_w:pool_w + gm_w] = (gm_buf[...] * rinv * ngg_ref[...]).astype(o_ref.dtype)

    off = pool_w + 2 * gm_w
    bg = z_ref[:, off:off + conv_w].astype(F32)
    x = z_ref[:, off + conv_w:off + 2 * conv_w].astype(F32) * z_ref[:, off + 2 * conv_w:off + 3 * conv_w].astype(F32)
    ext = jnp.concatenate([conv_carry[...], x], axis=0)
    conv_carry[...] = x[ts - V7X_SUBLANES:, :]
    zc = (wconv_ref[0:1, :] * pltpu.roll(ext, 2, axis=0)[V7X_SUBLANES:, :]
          + wconv_ref[1:2, :] * pltpu.roll(ext, 1, axis=0)[V7X_SUBLANES:, :]
          + wconv_ref[2:3, :] * x)
    y = bg * zc
    rinv = lax.rsqrt(jnp.mean(y * y, axis=-1, keepdims=True) + RMS_EPS)
    o_ref[:, pool_w + gm_w:pool_w + gm_w + conv_w] = (y * rinv * ngc_ref[...]).astype(o_ref.dtype)


def mixer(z, batch, seq, w_pool, pool_scale, ln_g, ln_b, w_spatial, b_spatial_t, w_conv, ng_gmlp, ng_conv):
    n_groups, gdim, _ = w_pool.shape
    pool_w = n_groups * gdim
    heads, chunk, _ = w_spatial.shape
    gm_w = heads * chunk
    conv_w = w_conv.shape[1]
    assert w_conv.shape[0] == 3 and n_groups == len(POOL_WINDOWS)
    width = z.shape[1]
    assert width == pool_w + 2 * gm_w + 3 * conv_w
    mix_w = pool_w + gm_w + conv_w
    ts = _tile(seq, 256, chunk)
    n_s = seq // ts
    full = lambda a: pl.BlockSpec(a.shape, lambda b, s: (0,) * a.ndim)
    kern = functools.partial(_mixer_kernel, dims=(pool_w, gdim, gm_w, chunk, conv_w))
    return pl.pallas_call(
        kern,
        grid=(batch, n_s),
        in_specs=[pl.BlockSpec((ts, width), lambda b, s: (b * n_s + s, 0)),
                  full(w_pool), full(pool_scale), full(ln_g), full(ln_b), full(w_spatial),
                  full(b_spatial_t), full(w_conv), full(ng_gmlp), full(ng_conv)],
        out_specs=pl.BlockSpec((ts, mix_w), lambda b, s: (b * n_s + s, 0)),
        out_shape=jax.ShapeDtypeStruct((batch * seq, mix_w), BF16),
        scratch_shapes=[pltpu.VMEM((MAX_HALO, pool_w), F32),
                        pltpu.VMEM((V7X_SUBLANES, conv_w), F32),
                        pltpu.VMEM((ts, gm_w), F32)],
        compiler_params=_params(("arbitrary", "arbitrary")),
        name="mixer",
    )(z, w_pool, pool_scale, ln_g, ln_b, w_spatial, b_spatial_t, w_conv, ng_gmlp, ng_conv)


def _layer_norm_store(acc, n_j, tn, n, g_ref, b_ref, o32_ref, o2_ref):
    tot = jnp.sum(acc[0], axis=-1, keepdims=True)
    for jj in range(1, n_j):
        tot = tot + jnp.sum(acc[jj], axis=-1, keepdims=True)
    mu = tot / n
    c0 = acc[0] - mu
    sq = jnp.sum(c0 * c0, axis=-1, keepdims=True)
    for jj in range(1, n_j):
        c = acc[jj] - mu
        sq = sq + jnp.sum(c * c, axis=-1, keepdims=True)
    rstd = lax.rsqrt(sq / n + LN_EPS)

    def norm(jj):
        sl = slice(jj * tn, (jj + 1) * tn)
        y = (acc[jj] - mu) * rstd * g_ref[:, sl] + b_ref[:, sl]
        o32_ref[:, sl] = y
        return y

    for jj in range(n_j):
        y = norm(jj)
        if len(o2_ref.shape) == 3:
            for c in range(tn // V7X_LANES):
                o2_ref[:, jj * (tn // V7X_LANES) + c, :] = y[:, c * V7X_LANES:(c + 1) * V7X_LANES]
        else:
            o2_ref[:, jj * tn:(jj + 1) * tn] = y.astype(BF16)


def _proj_ln_kernel(x_ref, w_ref, res_ref, g_ref, b_ref, o32_ref, o16_ref, acc, *, alpha):
    j = pl.program_id(1)
    n_j, _, tn = acc.shape
    y = jnp.dot(x_ref[...], w_ref[...], preferred_element_type=F32)
    acc[j] = alpha * res_ref[...] + y

    @pl.when(j == n_j - 1)
    def _():
        _layer_norm_store(acc, n_j, tn, n_j * tn, g_ref, b_ref, o32_ref, o16_ref)


def _ple_ln_kernel(x_ref, w_ref, bias_ref, p_ref, wp_ref, res_ref, g_ref, b_ref, o32_ref, o16_ref, acc,
                   *, alpha):
    j = pl.program_id(1)
    n_j, _, tn = acc.shape
    gate = _sigmoid(jnp.dot(x_ref[...], w_ref[...], preferred_element_type=F32) + bias_ref[...])
    proj = jnp.dot(p_ref[...].astype(BF16), wp_ref[...], preferred_element_type=F32)
    acc[j] = alpha * res_ref[...] + gate * proj

    @pl.when(j == n_j - 1)
    def _():
        _layer_norm_store(acc, n_j, tn, n_j * tn, g_ref, b_ref, o32_ref, o16_ref)


def proj_ln(x, w, res, ln_g, ln_b, alpha, ple=None, slabs=False):
    m, k = x.shape
    n = w.shape[1]
    tm = _tile(m, 512, V7X_SUBLANES * 2)
    tn = _tile(n, 512, V7X_LANES)
    n_j = n // tn
    x_spec = pl.BlockSpec((tm, k), lambda i, j: (i, 0))
    w_spec = pl.BlockSpec((k, tn), lambda i, j: (0, j))
    col_spec = pl.BlockSpec((tm, tn), lambda i, j: (i, j))
    vec_full = pl.BlockSpec((1, n), lambda i, j: (0, 0))
    row_spec = pl.BlockSpec((tm, n), lambda i, j: (i, 0), pipeline_mode=pl.Buffered(1))
    slab_spec = pl.BlockSpec((tm, n // V7X_LANES, V7X_LANES), lambda i, j: (i, 0, 0),
                             pipeline_mode=pl.Buffered(1))
    if ple is None:
        kern = functools.partial(_proj_ln_kernel, alpha=alpha)
        in_specs = [x_spec, w_spec, col_spec, vec_full, vec_full]
        args = (x, w, res, ln_g, ln_b)
        name = "out_proj_ln"
    else:
        bias, p, w_p = ple
        kp = p.shape[1]
        kern = functools.partial(_ple_ln_kernel, alpha=alpha)
        in_specs = [x_spec, w_spec, pl.BlockSpec((1, tn), lambda i, j: (0, j)),
                    pl.BlockSpec((tm, kp), lambda i, j: (i, 0)),
                    pl.BlockSpec((kp, tn), lambda i, j: (0, j)),
                    col_spec, vec_full, vec_full]
        args = (x, w, bias, p, w_p, res, ln_g, ln_b)
        name = "ple_ln"
    return pl.pallas_call(
        kern,
        grid=(m // tm, n_j),
        in_specs=in_specs,
        out_specs=[row_spec, slab_spec if slabs else row_spec],
        out_shape=[jax.ShapeDtypeStruct((m, n), F32),
                   jax.ShapeDtypeStruct((m, n // V7X_LANES, V7X_LANES), F32) if slabs
                   else jax.ShapeDtypeStruct((m, n), BF16)],
        scratch_shapes=[pltpu.VMEM((n_j, tm, tn), F32)],
        compiler_params=_params(("parallel", "arbitrary")),
        name=name,
    )(*args)


def _first_argmax(x, idx, axis, big):
    m = jnp.max(x, axis=axis, keepdims=True)
    first = jnp.min(jnp.where(x == m, idx, big), axis=axis, keepdims=True)
    return m, first


def _router_kernel(x_ref, wh_ref, wl_ref, bias_ref, eidx_ref, wts_ref, rank_ref, cnt_ref, carry):
    tm = x_ref.shape[0]
    n_e = wh_ref.shape[0]
    per = n_e // N_GROUPS
    i = pl.program_id(0)

    @pl.when(i == 0)
    def _():
        carry[...] = jnp.zeros_like(carry)

    x = x_ref[...]
    xh = x.astype(BF16)
    xl = (x - xh.astype(F32)).astype(BF16)
    nt = (((1,), (1,)), ((), ()))
    logits = (lax.dot_general(wh_ref[...], xh, nt, preferred_element_type=F32)
              + lax.dot_general(wh_ref[...], xl, nt, preferred_element_type=F32)
              + lax.dot_general(wl_ref[...], xh, nt, preferred_element_type=F32))
    scores = _sigmoid(logits)
    sel = scores + bias_ref[...]

    sel3 = sel.reshape(N_GROUPS, per, tm)
    j_idx = lax.broadcasted_iota(jnp.int32, (N_GROUPS, per, tm), 1).astype(F32)
    m1, a1 = _first_argmax(sel3, j_idx, 1, float(per))
    m2 = jnp.max(jnp.where(j_idx == a1, -jnp.inf, sel3), axis=1, keepdims=True)
    gscore = (m1 + m2).reshape(N_GROUPS, tm)
    g_idx = lax.broadcasted_iota(jnp.int32, (N_GROUPS, tm), 0).astype(F32)
    gmask = jnp.zeros((N_GROUPS, tm), F32)
    work = gscore
    for _ in range(TOPK_GROUPS):
        _, ga = _first_argmax(work, g_idx, 0, float(N_GROUPS))
        hit = g_idx == ga
        gmask = jnp.where(hit, 1.0, gmask)
        work = jnp.where(hit, -jnp.inf, work)
    e_idx = lax.broadcasted_iota(jnp.int32, (n_e, tm), 0).astype(F32)
    emask = jnp.broadcast_to(gmask.reshape(N_GROUPS, 1, tm), (N_GROUPS, per, tm)).reshape(n_e, tm)
    work = jnp.where(emask > 0.5, sel, -jnp.inf)

    e_rows, s_rows, hots = [], [], []
    for _ in range(TOP_K):
        _, ea = _first_argmax(work, e_idx, 0, float(n_e))
        hit = e_idx == ea
        work = jnp.where(hit, -jnp.inf, work)
        e_rows.append(ea)
        s_rows.append(jnp.sum(jnp.where(hit, scores, 0.0), axis=0, keepdims=True))
        hots.append(jnp.where(hit, 1.0, 0.0))
    e_sel = jnp.concatenate(e_rows, axis=0)
    s_sel = jnp.concatenate(s_rows, axis=0)
    eidx_ref[...] = e_sel.astype(jnp.int32)
    wts_ref[...] = ROUTED_SCALE * s_sel / jnp.sum(s_sel, axis=0, keepdims=True)

    hot = jnp.concatenate(hots, axis=0)
    r_i = lax.broadcasted_iota(jnp.int32, (tm, tm), 0)
    c_i = lax.broadcasted_iota(jnp.int32, (tm, tm), 1)
    upper = jnp.where(r_i < c_i, 1.0, 0.0).astype(BF16)
    prefix = jnp.dot(hot.astype(BF16), upper, preferred_element_type=F32)
    base = carry[...]
    ranks = []
    for k in range(TOP_K):
        hk = hots[k]
        pk = prefix[k * n_e:(k + 1) * n_e, :] + base
        ranks.append(jnp.sum(hk * pk, axis=0, keepdims=True))
        base = base + jnp.sum(hk, axis=1, keepdims=True)
    rank_ref[...] = jnp.concatenate(ranks, axis=0).astype(jnp.int32)
    carry[...] = base
    cnt_ref[...] = base.astype(jnp.int32)


def router(h, w_hi_t, w_lo_t, bias_col):
    t, d = h.shape
    n_e = w_hi_t.shape[0]
    tm = _tile(t, 256, V7X_LANES)
    full = lambda a: pl.BlockSpec(a.shape, lambda i: (0,) * a.ndim)
    tok_spec = pl.BlockSpec((TOP_K, tm), lambda i: (0, i))
    return pl.pallas_call(
        _router_kernel,
        grid=(t // tm,),
        in_specs=[pl.BlockSpec((tm, d), lambda i: (i, 0)), full(w_hi_t), full(w_lo_t), full(bias_col)],
        out_specs=[tok_spec, tok_spec, tok_spec, pl.BlockSpec((n_e, 1), lambda i: (0, 0))],
        out_shape=[jax.ShapeDtypeStruct((TOP_K, t), jnp.int32),
                   jax.ShapeDtypeStruct((TOP_K, t), F32),
                   jax.ShapeDtypeStruct((TOP_K, t), jnp.int32),
                   jax.ShapeDtypeStruct((n_e, 1), jnp.int32)],
        scratch_shapes=[pltpu.VMEM((n_e, 1), F32)],
        compiler_params=_params(("arbitrary",)),
        name="router",
    )(h, w_hi_t, w_lo_t, bias_col)


def _moe_kernel(te_ref, nu_ref, tok_cur, tok_next, dst_prev, x_hbm, wg_ref, wu_ref, wd_ref, y_hbm,
                x_in, x_bf, wgu, y_out, y_new, gsem, ssem):
    i = pl.program_id(0)
    n_used = nu_ref[0]
    tm, n_slab, lanes = x_in.shape
    f = wd_ref.shape[0]

    def gather_row(idx_ref, r):
        pltpu.make_async_copy(x_hbm.at[idx_ref[0, 0, r]], x_in.at[r], gsem).start()

    def scatter_row(r):
        pltpu.make_async_copy(y_out.at[pl.ds(r, 1)], y_hbm.at[pl.ds(dst_prev[0, 0, r], 1)], ssem).start()

    def wait_gather():
        pltpu.make_async_copy(x_hbm.at[pl.ds(0, tm)], x_in, gsem).wait()

    def wait_scatter():
        pltpu.make_async_copy(y_out, y_hbm.at[pl.ds(0, tm)], ssem).wait()

    def loop_rows(fn):
        def body(r, carry):
            fn(r)
            return carry
        lax.fori_loop(0, tm, body, 0)

    @pl.when(i == 0)
    def _():
        y_new[...] = jnp.zeros_like(y_new)
        loop_rows(lambda r: gather_row(tok_cur, r))

    @pl.when(jnp.logical_and(i >= 1, i <= n_used))
    def _():
        wait_scatter()

    def move_rows(r0, carry):
        rows = pl.ds(pl.multiple_of(r0 * 16, 16), 16)
        for s in range(n_slab):
            x_bf[rows, s * lanes:(s + 1) * lanes] = x_in[rows, s, :].astype(BF16)
        y_out[rows, :] = y_new[rows, :]
        return carry

    @pl.when(i < n_used)
    def _():
        wait_gather()
        lax.fori_loop(0, tm // 16, move_rows, 0)
        for r in range(tm):
            gather_row(tok_next, r)
            scatter_row(r)

    @pl.when(jnp.logical_and(i < n_used, jnp.logical_or(i == 0, te_ref[i] != te_ref[jnp.maximum(i - 1, 0)])))
    def _():
        def fuse(k0, carry):
            rows = pl.ds(pl.multiple_of(k0 * 256, 256), 256)
            wgu[rows, 0:f] = wg_ref[rows, :]
            wgu[rows, f:2 * f] = wu_ref[rows, :]
            return carry
        lax.fori_loop(0, wgu.shape[0] // 256, fuse, 0)

    @pl.when(jnp.logical_and(i < n_used, te_ref[i] >= 0))
    def _():
        rows = tm // MOE_ROW_GROUPS
        for h in range(MOE_ROW_GROUPS):
            rs = slice(h * rows, (h + 1) * rows)
            gu = jnp.dot(x_bf[rs, :], wgu[...], preferred_element_type=F32)
            g = gu[:, 0:f]
            a = (g * _sigmoid(g) * gu[:, f:2 * f]).astype(BF16)
            y_new[rs, :] = jnp.dot(a, wd_ref[...], preferred_element_type=F32)

    @pl.when(i == n_used)
    def _():
        wait_gather()
        lax.fori_loop(0, tm // 16, move_rows, 0)
        loop_rows(scatter_row)
        wait_scatter()


def moe_grouped(x_slabs, dst_all, tile_expert, n_used, wg, wu, wd, layer, tm, out_rows):
    n_grid = tile_expert.shape[0]
    t, n_slab, lanes = x_slabs.shape
    d = n_slab * lanes
    f = wd.shape[2]
    dst3 = dst_all.reshape(n_grid + 1, 1, tm)
    tok3 = dst3 % t
    smem_blk = lambda fn: pl.BlockSpec((1, 1, tm), fn, memory_space=pltpu.SMEM)
    w_up_spec = pl.BlockSpec((None, None, d, f), lambda i, te, nu: (layer, te[i], 0, 0))
    grid_spec = pltpu.PrefetchScalarGridSpec(
        num_scalar_prefetch=2,
        grid=(n_grid,),
        in_specs=[smem_blk(lambda i, te, nu: (i + 1, 0, 0)),
                  smem_blk(lambda i, te, nu: (jnp.minimum(i + 2, n_grid), 0, 0)),
                  smem_blk(lambda i, te, nu: (i, 0, 0)),
                  pl.BlockSpec(memory_space=pl.ANY),
                  w_up_spec, w_up_spec,
                  pl.BlockSpec((None, None, f, d), lambda i, te, nu: (layer, te[i], 0, 0))],
        out_specs=pl.BlockSpec(memory_space=pl.ANY),
        scratch_shapes=[pltpu.VMEM((tm, n_slab, lanes), F32), pltpu.VMEM((tm, d), BF16),
                        pltpu.VMEM((d, 2 * f), BF16),
                        pltpu.VMEM((tm, d), F32), pltpu.VMEM((tm, d), F32),
                        pltpu.SemaphoreType.DMA(()), pltpu.SemaphoreType.DMA(())],
    )
    return pl.pallas_call(
        _moe_kernel,
        grid_spec=grid_spec,
        out_shape=jax.ShapeDtypeStruct((out_rows, d), F32),
        compiler_params=_params(("arbitrary",)),
        name="moe_grouped",
    )(tile_expert, n_used, tok3, tok3, dst3, x_slabs, wg, wu, wd)


def _combine_kernel(wts_ref, h_ref, sh_ref, *rest, alpha):
    planes = rest[:TOP_K]
    g_ref, b_ref, o32_ref, o16_ref = rest[TOP_K:]
    acc = alpha * h_ref[...] + sh_ref[...]
    for k in range(TOP_K):
        acc = acc + wts_ref[:, k:k + 1] * planes[k][...]
    mu = jnp.mean(acc, axis=-1, keepdims=True)
    c = acc - mu
    var = jnp.mean(c * c, axis=-1, keepdims=True)
    y = c * lax.rsqrt(var + LN_EPS) * g_ref[...] + b_ref[...]
    o32_ref[...] = y
    o16_ref[...] = y.astype(BF16)


def combine_ln(h, y_shared, y_slots, wts, ln_g, ln_b, alpha):
    t, d = h.shape
    tm = _tile(t, 128, V7X_SUBLANES * 2)
    n_i = t // tm
    row = pl.BlockSpec((tm, d), lambda i: (i, 0))
    vec = pl.BlockSpec((1, d), lambda i: (0, 0))
    plane = lambda k: pl.BlockSpec((tm, d), lambda i: (k * n_i + i, 0))
    return pl.pallas_call(
        functools.partial(_combine_kernel, alpha=alpha),
        grid=(n_i,),
        in_specs=[pl.BlockSpec((tm, TOP_K), lambda i: (i, 0)), row, plane(0)]
        + [plane(k) for k in range(TOP_K)] + [vec, vec],
        out_specs=[row, row],
        out_shape=[jax.ShapeDtypeStruct((t, d), F32), jax.ShapeDtypeStruct((t, d), BF16)],
        compiler_params=_params(("parallel",)),
        name="combine_ln",
    )(wts, h, y_shared, *([y_slots] * TOP_K), ln_g, ln_b)


def _dispatch_plan(eidx_t, rank_t, counts, tm, n_grid):
    t = eidx_t.shape[1]
    n_e = counts.shape[0]
    padded = ((counts + tm - 1) // tm) * tm
    ends = jnp.cumsum(padded)
    starts = ends - padded
    hit = eidx_t[:, :, None] == jnp.arange(n_e, dtype=jnp.int32)
    pos = jnp.sum(jnp.where(hit, starts, 0), axis=-1) + rank_t
    slot = (jnp.arange(TOP_K, dtype=jnp.int32)[:, None] * t + jnp.arange(t, dtype=jnp.int32)[None, :])
    spare = TOP_K * t + jnp.arange((n_grid + 1) * tm, dtype=jnp.int32) % tm
    dst_all = spare.at[(tm + pos).reshape(-1)].set(slot.reshape(-1), unique_indices=True,
                                                   mode="promise_in_bounds")
    tile_start = jnp.arange(n_grid, dtype=jnp.int32) * tm
    tile_expert = jnp.minimum(jnp.sum(ends[None, :] <= tile_start[:, None], axis=1), n_e - 1).astype(jnp.int32)
    n_used = (ends[-1] // tm).astype(jnp.int32).reshape(1)
    return dst_all, tile_expert, n_used


def kernel(x, p, w_in, w_pool, pool_scale, gmlp_ln_g, gmlp_ln_b, w_spatial, b_spatial, w_conv, norm_g_gmlp,
           norm_g_conv, w_out, ln1_g, ln1_b, w_router, router_bias, w_exp_gate, w_exp_up, w_exp_down,
           w_sh_gate, w_sh_up, w_sh_down, ln2_g, ln2_b, w_ple_gate, b_ple_gate, w_ple_proj, ln3_g, ln3_b):
    depth = w_in.shape[0]
    batch, seq, d = x.shape
    t = batch * seq
    n_e = w_router.shape[2]
    alpha = (2.0 * depth) ** 0.25
    row = lambda a: a.reshape(1, -1)

    tm_e = _tile(t * TOP_K // n_e, 256, V7X_SUBLANES * 2)
    n_grid = t * TOP_K // tm_e + n_e
    tm_s = _tile(t, 256, V7X_SUBLANES * 2)
    n_grid_s = t // tm_s + 1
    spare_s = t + jnp.arange(tm_s, dtype=jnp.int32)
    dst_shared = jnp.concatenate([spare_s, jnp.arange(t, dtype=jnp.int32), spare_s])
    shared_tiles = jnp.zeros((n_grid_s,), jnp.int32)
    shared_used = jnp.full((1,), t // tm_s, jnp.int32)

    wg16, wu16, wd16 = w_exp_gate.astype(BF16), w_exp_up.astype(BF16), w_exp_down.astype(BF16)
    sg16, su16, sd16 = (w.astype(BF16)[:, None] for w in (w_sh_gate, w_sh_up, w_sh_down))

    h32 = x.reshape(t, d)
    h16 = h32.astype(BF16)
    p2 = p.reshape(depth, t, p.shape[-1])
    for i in range(depth):
        z = matmul(h16, w_in[i].astype(BF16), BF16)
        ycat = mixer(z, batch, seq, w_pool[i].astype(BF16), row(pool_scale[i]), row(gmlp_ln_g[i]),
                     row(gmlp_ln_b[i]), w_spatial[i], b_spatial[i].T, w_conv[i], row(norm_g_gmlp[i]),
                     row(norm_g_conv[i]))
        h32, hsl = proj_ln(ycat, w_out[i].astype(BF16), h32, row(ln1_g[i]), row(ln1_b[i]), alpha, slabs=True)
        wr_t = w_router[i].T
        wr_hi = wr_t.astype(BF16)
        wr_lo = (wr_t - wr_hi.astype(F32)).astype(BF16)
        eidx_t, wts_t, rank_t, counts = router(h32, wr_hi, wr_lo, router_bias[i].reshape(n_e, 1))
        dst_all, tile_expert, n_used = _dispatch_plan(eidx_t, rank_t, counts[:, 0], tm_e, n_grid)
        y_slots = moe_grouped(hsl, dst_all, tile_expert, n_used, wg16, wu16, wd16, i, tm_e, TOP_K * t + tm_e)
        y_shared = moe_grouped(hsl, dst_shared, shared_tiles, shared_used, sg16, su16, sd16, i, tm_s, t + tm_s)
        h32, h16 = combine_ln(h32, y_shared, y_slots, wts_t.T, row(ln2_g[i]), row(ln2_b[i]), alpha)
        h32, h16 = proj_ln(h16, w_ple_gate[i].astype(BF16), h32, row(ln3_g[i]), row(ln3_b[i]), alpha,
                           ple=(row(b_ple_gate[i]), p2[i], w_ple_proj[i].astype(BF16)))
    return h32.reshape(batch, seq, d)
```

```python
import functools
import math

import jax
import jax.numpy as jnp
from jax import lax
from jax.experimental import pallas as pl
from jax.experimental.pallas import tpu as pltpu

F32 = jnp.float32
BF16 = jnp.bfloat16

POOL_WINDOWS = (2, 4, 8, 16)
TOP_K = 8
N_GROUPS = 8
TOPK_GROUPS = 4
ROUTED_SCALE = 2.5
LN_EPS = 1e-5
RMS_EPS = 1e-6

V7X_LANES = 128
V7X_SUBLANES = 8
V7X_VMEM_LIMIT_BYTES = 60 * 1024 * 1024
MAX_HALO = 16
MOE_ROW_GROUPS = 2
def _tile(n, pref, align):
    t = min(pref, n)
    t -= t % align
    while t >= align:
        if n % t == 0:
            return t
        t -= align
    return n


def _params(sem, flags=None):
    return pltpu.CompilerParams(dimension_semantics=sem, vmem_limit_bytes=V7X_VMEM_LIMIT_BYTES, flags=flags)


def _sigmoid(x):
    return 1.0 / (1.0 + jnp.exp(-x))


def _matmul_kernel(x_ref, w_ref, o_ref):
    o_ref[...] = jnp.dot(x_ref[...], w_ref[...], preferred_element_type=F32).astype(o_ref.dtype)


def matmul(x, w, out_dtype):
    m, k = x.shape
    n = w.shape[1]
    tm = _tile(m, 1024, V7X_SUBLANES * 2)
    tn = _tile(n, 512, V7X_LANES)
    return pl.pallas_call(
        _matmul_kernel,
        grid=(m // tm, n // tn),
        in_specs=[pl.BlockSpec((tm, k), lambda i, j: (i, 0)),
                  pl.BlockSpec((k, tn), lambda i, j: (0, j))],
        out_specs=pl.BlockSpec((tm, tn), lambda i, j: (i, j)),
        out_shape=jax.ShapeDtypeStruct((m, n), out_dtype),
        compiler_params=_params(("parallel", "arbitrary")),
        name="in_proj",
    )(x, w)


def _mixer_kernel(z_ref, wpool_ref, pscale_ref, lng_ref, lnb_ref, ws_ref, bs_ref, wconv_ref,
                  ngg_ref, ngc_ref, o_ref, pool_carry, conv_carry, gm_buf, *, dims):
    pool_w, gdim, gm_w, chunk, conv_w = dims
    ts = z_ref.shape[0]
    s = pl.program_id(1)
    n_groups = pool_w // gdim
    heads = gm_w // chunk

    @pl.when(s == 0)
    def _():
        pool_carry[...] = jnp.zeros_like(pool_carry)
        conv_carry[...] = jnp.zeros_like(conv_carry)

    pos = (s * ts + lax.broadcasted_iota(jnp.int32, (ts, 1), 0)).astype(F32)

    a = z_ref[:, 0:pool_w].astype(F32)
    ext = jnp.concatenate([pool_carry[...], a], axis=0)
    pool_carry[...] = a[ts - MAX_HALO:, :]
    ys = []
    ssq = jnp.zeros((ts, 1), F32)
    for g in range(n_groups):
        win = POOL_WINDOWS[g]
        acc = ext[:, g * gdim:(g + 1) * gdim]
        span = 1
        while span < win:
            acc = acc + pltpu.roll(acc, span, axis=0)
            span *= 2
        count = jnp.minimum(pos + 1.0, float(win))
        d = acc[MAX_HALO:, :] / count - a[:, g * gdim:(g + 1) * gdim]
        y = jnp.dot(d.astype(BF16), wpool_ref[g], preferred_element_type=F32)
        ssq = ssq + jnp.sum(y * y, axis=-1, keepdims=True)
        ys.append(y)
    rinv = lax.rsqrt(ssq / pool_w + RMS_EPS)
    for g in range(n_groups):
        o_ref[:, g * gdim:(g + 1) * gdim] = (
            ys[g] * rinv * pscale_ref[:, g * gdim:(g + 1) * gdim]).astype(o_ref.dtype)

    uv = jax.nn.gelu(z_ref[:, pool_w:pool_w + 2 * gm_w].astype(F32))
    u = uv[:, :gm_w]
    v = uv[:, gm_w:]
    mu = jnp.mean(v, axis=-1, keepdims=True)
    vc = v - mu
    var = jnp.mean(vc * vc, axis=-1, keepdims=True)
    vn = (vc * lax.rsqrt(var + LN_EPS) * lng_ref[...] + lnb_ref[...]).astype(BF16)
    row = lax.broadcasted_iota(jnp.int32, (chunk, chunk), 0)
    col = lax.broadcasted_iota(jnp.int32, (chunk, chunk), 1)
    ssq = jnp.zeros((ts, 1), F32)
    for h in range(heads):
        wsm = jnp.where(row >= col, ws_ref[h], 0.0).astype(BF16)
        bias = bs_ref[:, h:h + 1]
        parts = []
        for c in range(ts // chunk):
            blk = vn[c * chunk:(c + 1) * chunk, h * chunk:(h + 1) * chunk]
            parts.append(jnp.dot(wsm, blk, preferred_element_type=F32) + bias)
        sv = jnp.concatenate(parts, axis=0) if len(parts) > 1 else parts[0]
        y = u[:, h * chunk:(h + 1) * chunk] * sv
        ssq = ssq + jnp.sum(y * y, axis=-1, keepdims=True)
        gm_buf[:, h * chunk:(h + 1) * chunk] = y
    rinv = lax.rsqrt(ssq / gm_w + RMS_EPS)
    o_ref[:, pool_w:pool_w + gm_w] = (gm_buf[...] * rinv * ngg_ref[...]).astype(o_ref.dtype)

    off = pool_w + 2 * gm_w
    bg = z_ref[:, off:off + conv_w].astype(F32)
    x = z_ref[:, off + conv_w:off + 2 * conv_w].astype(F32) * z_ref[:, off + 2 * conv_w:off + 3 * conv_w].astype(F32)
    ext = jnp.concatenate([conv_carry[...], x], axis=0)
    conv_carry[...] = x[ts - V7X_SUBLANES:, :]
    zc = (wconv_ref[0:1, :] * pltpu.roll(ext, 2, axis=0)[V7X_SUBLANES:, :]
          + wconv_ref[1:2, :] * pltpu.roll(ext, 1, axis=0)[V7X_SUBLANES:, :]
          + wconv_ref[2:3, :] * x)
    y = bg * zc
    rinv = lax.rsqrt(jnp.mean(y * y, axis=-1, keepdims=True) + RMS_EPS)
    o_ref[:, pool_w + gm_w:pool_w + gm_w + conv_w] = (y * rinv * ngc_ref[...]).astype(o_ref.dtype)


def mixer(z, batch, seq, w_pool, pool_scale, ln_g, ln_b, w_spatial, b_spatial_t, w_conv, ng_gmlp, ng_conv):
    n_groups, gdim, _ = w_pool.shape
    pool_w = n_groups * gdim
    heads, chunk, _ = w_spatial.shape
    gm_w = heads * chunk
    conv_w = w_conv.shape[1]
    assert w_conv.shape[0] == 3 and n_groups == len(POOL_WINDOWS)
    width = z.shape[1]
    assert width == pool_w + 2 * gm_w + 3 * conv_w
    mix_w = pool_w + gm_w + conv_w
    ts = _tile(seq, 256, chunk)
    n_s = seq // ts
    full = lambda a: pl.BlockSpec(a.shape, lambda b, s: (0,) * a.ndim)
    kern = functools.partial(_mixer_kernel, dims=(pool_w, gdim, gm_w, chunk, conv_w))
    return pl.pallas_call(
        kern,
        grid=(batch, n_s),
        in_specs=[pl.BlockSpec((ts, width), lambda b, s: (b * n_s + s, 0)),
                  full(w_pool), full(pool_scale), full(ln_g), full(ln_b), full(w_spatial),
                  full(b_spatial_t), full(w_conv), full(ng_gmlp), full(ng_conv)],
        out_specs=pl.BlockSpec((ts, mix_w), lambda b, s: (b * n_s + s, 0)),
        out_shape=jax.ShapeDtypeStruct((batch * seq, mix_w), BF16),
        scratch_shapes=[pltpu.VMEM((MAX_HALO, pool_w), F32),
                        pltpu.VMEM((V7X_SUBLANES, conv_w), F32),
                        pltpu.VMEM((ts, gm_w), F32)],
        compiler_params=_params(("arbitrary", "arbitrary")),
        name="mixer",
    )(z, w_pool, pool_scale, ln_g, ln_b, w_spatial, b_spatial_t, w_conv, ng_gmlp, ng_conv)


def _layer_norm_store(acc, n_j, tn, n, g_ref, b_ref, o32_ref, o16_ref):
    tot = jnp.sum(acc[0], axis=-1, keepdims=True)
    for jj in range(1, n_j):
        tot = tot + jnp.sum(acc[jj], axis=-1, keepdims=True)
    mu = tot / n
    c0 = acc[0] - mu
    sq = jnp.sum(c0 * c0, axis=-1, keepdims=True)
    for jj in range(1, n_j):
        c = acc[jj] - mu
        sq = sq + jnp.sum(c * c, axis=-1, keepdims=True)
    rstd = lax.rsqrt(sq / n + LN_EPS)

    for jj in range(n_j):
        sl = slice(jj * tn, (jj + 1) * tn)
        y = (acc[jj] - mu) * rstd * g_ref[:, sl] + b_ref[:, sl]
        o32_ref[:, sl] = y
        o16_ref[:, sl] = y.astype(BF16)


def _proj_ln_kernel(x_ref, w_ref, res_ref, g_ref, b_ref, o32_ref, o16_ref, acc, *, alpha):
    j = pl.program_id(1)
    n_j, _, tn = acc.shape
    y = jnp.dot(x_ref[...], w_ref[...], preferred_element_type=F32)
    acc[j] = alpha * res_ref[...] + y

    @pl.when(j == n_j - 1)
    def _():
        _layer_norm_store(acc, n_j, tn, n_j * tn, g_ref, b_ref, o32_ref, o16_ref)


def _ple_ln_kernel(x_ref, w_ref, bias_ref, p_ref, wp_ref, res_ref, g_ref, b_ref, o32_ref, o16_ref, acc,
                   *, alpha):
    j = pl.program_id(1)
    n_j, _, tn = acc.shape
    gate = _sigmoid(jnp.dot(x_ref[...], w_ref[...], preferred_element_type=F32) + bias_ref[...])
    proj = jnp.dot(p_ref[...].astype(BF16), wp_ref[...], preferred_element_type=F32)
    acc[j] = alpha * res_ref[...] + gate * proj

    @pl.when(j == n_j - 1)
    def _():
        _layer_norm_store(acc, n_j, tn, n_j * tn, g_ref, b_ref, o32_ref, o16_ref)


def proj_ln(x, w, res, ln_g, ln_b, alpha, ple=None):
    m, k = x.shape
    n = w.shape[1]
    tm = _tile(m, 512, V7X_SUBLANES * 2)
    tn = _tile(n, 512, V7X_LANES)
    n_j = n // tn
    x_spec = pl.BlockSpec((tm, k), lambda i, j: (i, 0))
    w_spec = pl.BlockSpec((k, tn), lambda i, j: (0, j))
    col_spec = pl.BlockSpec((tm, tn), lambda i, j: (i, j))
    vec_full = pl.BlockSpec((1, n), lambda i, j: (0, 0))
    row_spec = pl.BlockSpec((tm, n), lambda i, j: (i, 0), pipeline_mode=pl.Buffered(1))
    if ple is None:
        kern = functools.partial(_proj_ln_kernel, alpha=alpha)
        in_specs = [x_spec, w_spec, col_spec, vec_full, vec_full]
        args = (x, w, res, ln_g, ln_b)
        name = "out_proj_ln"
    else:
        bias, p, w_p = ple
        kp = p.shape[1]
        kern = functools.partial(_ple_ln_kernel, alpha=alpha)
        in_specs = [x_spec, w_spec, pl.BlockSpec((1, tn), lambda i, j: (0, j)),
                    pl.BlockSpec((tm, kp), lambda i, j: (i, 0)),
                    pl.BlockSpec((kp, tn), lambda i, j: (0, j)),
                    col_spec, vec_full, vec_full]
        args = (x, w, bias, p, w_p, res, ln_g, ln_b)
        name = "ple_ln"
    return pl.pallas_call(
        kern,
        grid=(m // tm, n_j),
        in_specs=in_specs,
        out_specs=[row_spec, row_spec],
        out_shape=[jax.ShapeDtypeStruct((m, n), F32), jax.ShapeDtypeStruct((m, n), BF16)],
        scratch_shapes=[pltpu.VMEM((n_j, tm, tn), F32)],
        compiler_params=_params(("parallel", "arbitrary")),
        name=name,
    )(*args)


def _first_argmax(x, idx, axis, big):
    m = jnp.max(x, axis=axis, keepdims=True)
    first = jnp.min(jnp.where(x == m, idx, big), axis=axis, keepdims=True)
    return m, first


def _router_kernel(x_ref, wh_ref, wl_ref, bias_ref, eidx_ref, wts_ref, rank_ref, cnt_ref, carry):
    tm = x_ref.shape[0]
    n_e = wh_ref.shape[0]
    per = n_e // N_GROUPS
    i = pl.program_id(0)

    @pl.when(i == 0)
    def _():
        carry[...] = jnp.zeros_like(carry)

    x = x_ref[...]
    xh = x.astype(BF16)
    xl = (x - xh.astype(F32)).astype(BF16)
    nt = (((1,), (1,)), ((), ()))
    logits = (lax.dot_general(wh_ref[...], xh, nt, preferred_element_type=F32)
              + lax.dot_general(wh_ref[...], xl, nt, preferred_element_type=F32)
              + lax.dot_general(wl_ref[...], xh, nt, preferred_element_type=F32))
    scores = _sigmoid(logits)
    sel = scores + bias_ref[...]

    sel3 = sel.reshape(N_GROUPS, per, tm)
    j_idx = lax.broadcasted_iota(jnp.int32, (N_GROUPS, per, tm), 1).astype(F32)
    m1, a1 = _first_argmax(sel3, j_idx, 1, float(per))
    m2 = jnp.max(jnp.where(j_idx == a1, -jnp.inf, sel3), axis=1, keepdims=True)
    gscore = (m1 + m2).reshape(N_GROUPS, tm)
    g_idx = lax.broadcasted_iota(jnp.int32, (N_GROUPS, tm), 0).astype(F32)
    gmask = jnp.zeros((N_GROUPS, tm), F32)
    work = gscore
    for _ in range(TOPK_GROUPS):
        _, ga = _first_argmax(work, g_idx, 0, float(N_GROUPS))
        hit = g_idx == ga
        gmask = jnp.where(hit, 1.0, gmask)
        work = jnp.where(hit, -jnp.inf, work)
    e_idx = lax.broadcasted_iota(jnp.int32, (n_e, tm), 0).astype(F32)
    emask = jnp.broadcast_to(gmask.reshape(N_GROUPS, 1, tm), (N_GROUPS, per, tm)).reshape(n_e, tm)
    work = jnp.where(emask > 0.5, sel, -jnp.inf)

    e_rows, s_rows, hots = [], [], []
    for _ in range(TOP_K):
        _, ea = _first_argmax(work, e_idx, 0, float(n_e))
        hit = e_idx == ea
        work = jnp.where(hit, -jnp.inf, work)
        e_rows.append(ea)
        s_rows.append(jnp.sum(jnp.where(hit, scores, 0.0), axis=0, keepdims=True))
        hots.append(jnp.where(hit, 1.0, 0.0))
    e_sel = jnp.concatenate(e_rows, axis=0)
    s_sel = jnp.concatenate(s_rows, axis=0)
    eidx_ref[...] = e_sel.astype(jnp.int32)
    wts_ref[...] = ROUTED_SCALE * s_sel / jnp.sum(s_sel, axis=0, keepdims=True)

    hot = jnp.concatenate(hots, axis=0)
    r_i = lax.broadcasted_iota(jnp.int32, (tm, tm), 0)
    c_i = lax.broadcasted_iota(jnp.int32, (tm, tm), 1)
    upper = jnp.where(r_i < c_i, 1.0, 0.0).astype(BF16)
    prefix = jnp.dot(hot.astype(BF16), upper, preferred_element_type=F32)
    base = carry[...]
    ranks = []
    for k in range(TOP_K):
        hk = hots[k]
        pk = prefix[k * n_e:(k + 1) * n_e, :] + base
        ranks.append(jnp.sum(hk * pk, axis=0, keepdims=True))
        base = base + jnp.sum(hk, axis=1, keepdims=True)
    rank_ref[...] = jnp.concatenate(ranks, axis=0).astype(jnp.int32)
    carry[...] = base
    cnt_ref[...] = base.astype(jnp.int32)


def router(h, w_hi_t, w_lo_t, bias_col):
    t, d = h.shape
    n_e = w_hi_t.shape[0]
    tm = _tile(t, 256, V7X_LANES)
    full = lambda a: pl.BlockSpec(a.shape, lambda i: (0,) * a.ndim)
    tok_spec = pl.BlockSpec((TOP_K, tm), lambda i: (0, i))
    return pl.pallas_call(
        _router_kernel,
        grid=(t // tm,),
        in_specs=[pl.BlockSpec((tm, d), lambda i: (i, 0)), full(w_hi_t), full(w_lo_t), full(bias_col)],
        out_specs=[tok_spec, tok_spec, tok_spec, pl.BlockSpec((n_e, 1), lambda i: (0, 0))],
        out_shape=[jax.ShapeDtypeStruct((TOP_K, t), jnp.int32),
                   jax.ShapeDtypeStruct((TOP_K, t), F32),
                   jax.ShapeDtypeStruct((TOP_K, t), jnp.int32),
                   jax.ShapeDtypeStruct((n_e, 1), jnp.int32)],
        scratch_shapes=[pltpu.VMEM((n_e, 1), F32)],
        compiler_params=_params(("arbitrary",)),
        name="router",
    )(h, w_hi_t, w_lo_t, bias_col)


def _moe_kernel(te_ref, nu_ref, tok_cur, tok_next, dst_prev, x_hbm, wg_ref, wu_ref, wd_ref, y_hbm,
                x_in, x_bf, wgu, y_out, y_new, gsem, ssem):
    i = pl.program_id(0)
    n_used = nu_ref[0]
    tm = x_in.shape[0]
    f = wd_ref.shape[0]

    def gather_row(idx_ref, r):
        pltpu.make_async_copy(x_hbm.at[pl.ds(idx_ref[0, 0, r], 1)], x_in.at[pl.ds(r, 1)], gsem).start()

    def scatter_row(r):
        pltpu.make_async_copy(y_out.at[pl.ds(r, 1)], y_hbm.at[pl.ds(dst_prev[0, 0, r], 1)], ssem).start()

    def wait_gather():
        pltpu.make_async_copy(x_hbm.at[pl.ds(0, tm)], x_in, gsem).wait()

    def wait_scatter():
        pltpu.make_async_copy(y_out, y_hbm.at[pl.ds(0, tm)], ssem).wait()

    def loop_rows(fn):
        def body(r, carry):
            fn(r)
            return carry
        lax.fori_loop(0, tm, body, 0)

    @pl.when(i == 0)
    def _():
        y_new[...] = jnp.zeros_like(y_new)
        loop_rows(lambda r: gather_row(tok_cur, r))

    @pl.when(jnp.logical_and(i >= 1, i <= n_used))
    def _():
        wait_scatter()

    def move_rows(r0, carry):
        rows = pl.ds(pl.multiple_of(r0 * 16, 16), 16)
        x_bf[rows, :] = x_in[rows, :].astype(BF16)
        y_out[rows, :] = y_new[rows, :]
        return carry

    @pl.when(i < n_used)
    def _():
        wait_gather()
        lax.fori_loop(0, tm // 16, move_rows, 0)
        for r in range(tm):
            gather_row(tok_next, r)
            scatter_row(r)

    @pl.when(jnp.logical_and(i < n_used, jnp.logical_or(i == 0, te_ref[i] != te_ref[jnp.maximum(i - 1, 0)])))
    def _():
        def fuse(k0, carry):
            rows = pl.ds(pl.multiple_of(k0 * 256, 256), 256)
            wgu[rows, 0:f] = wg_ref[rows, :]
            wgu[rows, f:2 * f] = wu_ref[rows, :]
            return carry
        lax.fori_loop(0, wgu.shape[0] // 256, fuse, 0)

    @pl.when(jnp.logical_and(i < n_used, te_ref[i] >= 0))
    def _():
        rows = tm // MOE_ROW_GROUPS
        for h in range(MOE_ROW_GROUPS):
            rs = slice(h * rows, (h + 1) * rows)
            gu = jnp.dot(x_bf[rs, :], wgu[...], preferred_element_type=F32)
            g = gu[:, 0:f]
            a = (g * _sigmoid(g) * gu[:, f:2 * f]).astype(BF16)
            y_new[rs, :] = jnp.dot(a, wd_ref[...], preferred_element_type=F32)

    @pl.when(i == n_used)
    def _():
        wait_gather()
        lax.fori_loop(0, tm // 16, move_rows, 0)
        loop_rows(scatter_row)
        wait_scatter()


def moe_grouped(x, dst_all, tile_expert, n_used, wg, wu, wd, layer, tm, out_rows):
    n_grid = tile_expert.shape[0]
    t, d = x.shape
    f = wd.shape[2]
    dst3 = dst_all.reshape(n_grid + 1, 1, tm)
    tok3 = dst3 % t
    smem_blk = lambda fn: pl.BlockSpec((1, 1, tm), fn, memory_space=pltpu.SMEM)
    w_up_spec = pl.BlockSpec((None, None, d, f), lambda i, te, nu: (layer, te[i], 0, 0))
    grid_spec = pltpu.PrefetchScalarGridSpec(
        num_scalar_prefetch=2,
        grid=(n_grid,),
        in_specs=[smem_blk(lambda i, te, nu: (i + 1, 0, 0)),
                  smem_blk(lambda i, te, nu: (jnp.minimum(i + 2, n_grid), 0, 0)),
                  smem_blk(lambda i, te, nu: (i, 0, 0)),
                  pl.BlockSpec(memory_space=pl.ANY),
                  w_up_spec, w_up_spec,
                  pl.BlockSpec((None, None, f, d), lambda i, te, nu: (layer, te[i], 0, 0))],
        out_specs=pl.BlockSpec(memory_space=pl.ANY),
        scratch_shapes=[pltpu.VMEM((tm, d), F32), pltpu.VMEM((tm, d), BF16), pltpu.VMEM((d, 2 * f), BF16),
                        pltpu.VMEM((tm, d), F32), pltpu.VMEM((tm, d), F32),
                        pltpu.SemaphoreType.DMA(()), pltpu.SemaphoreType.DMA(())],
    )
    return pl.pallas_call(
        _moe_kernel,
        grid_spec=grid_spec,
        out_shape=jax.ShapeDtypeStruct((out_rows, d), F32),
        compiler_params=_params(("arbitrary",)),
        name="moe_grouped",
    )(tile_expert, n_used, tok3, tok3, dst3, x, wg, wu, wd)


def _combine_kernel(wts_ref, h_ref, sh_ref, *rest, alpha):
    planes = rest[:TOP_K]
    g_ref, b_ref, o32_ref, o16_ref = rest[TOP_K:]
    acc = alpha * h_ref[...] + sh_ref[...]
    for k in range(TOP_K):
        acc = acc + wts_ref[:, k:k + 1] * planes[k][...]
    mu = jnp.mean(acc, axis=-1, keepdims=True)
    c = acc - mu
    var = jnp.mean(c * c, axis=-1, keepdims=True)
    y = c * lax.rsqrt(var + LN_EPS) * g_ref[...] + b_ref[...]
    o32_ref[...] = y
    o16_ref[...] = y.astype(BF16)


def combine_ln(h, y_shared, y_slots, wts, ln_g, ln_b, alpha):
    t, d = h.shape
    tm = _tile(t, 128, V7X_SUBLANES * 2)
    n_i = t // tm
    row = pl.BlockSpec((tm, d), lambda i: (i, 0))
    vec = pl.BlockSpec((1, d), lambda i: (0, 0))
    plane = lambda k: pl.BlockSpec((tm, d), lambda i: (k * n_i + i, 0))
    return pl.pallas_call(
        functools.partial(_combine_kernel, alpha=alpha),
        grid=(n_i,),
        in_specs=[pl.BlockSpec((tm, TOP_K), lambda i: (i, 0)), row, plane(0)]
        + [plane(k) for k in range(TOP_K)] + [vec, vec],
        out_specs=[row, row],
        out_shape=[jax.ShapeDtypeStruct((t, d), F32), jax.ShapeDtypeStruct((t, d), BF16)],
        compiler_params=_params(("parallel",)),
        name="combine_ln",
    )(wts, h, y_shared, *([y_slots] * TOP_K), ln_g, ln_b)


def _dispatch_plan(eidx_t, rank_t, counts, tm, n_grid):
    t = eidx_t.shape[1]
    n_e = counts.shape[0]
    padded = ((counts + tm - 1) // tm) * tm
    ends = jnp.cumsum(padded)
    starts = ends - padded
    hit = eidx_t[:, :, None] == jnp.arange(n_e, dtype=jnp.int32)
    pos = jnp.sum(jnp.where(hit, starts, 0), axis=-1) + rank_t
    slot = (jnp.arange(TOP_K, dtype=jnp.int32)[:, None] * t + jnp.arange(t, dtype=jnp.int32)[None, :])
    spare = TOP_K * t + jnp.arange((n_grid + 1) * tm, dtype=jnp.int32) % tm
    dst_all = spare.at[(tm + pos).reshape(-1)].set(slot.reshape(-1), unique_indices=True,
                                                   mode="promise_in_bounds")
    tile_start = jnp.arange(n_grid, dtype=jnp.int32) * tm
    tile_expert = jnp.minimum(jnp.sum(ends[None, :] <= tile_start[:, None], axis=1), n_e - 1).astype(jnp.int32)
    n_used = (ends[-1] // tm).astype(jnp.int32).reshape(1)
    return dst_all, tile_expert, n_used


def kernel(x, p, w_in, w_pool, pool_scale, gmlp_ln_g, gmlp_ln_b, w_spatial, b_spatial, w_conv, norm_g_gmlp,
           norm_g_conv, w_out, ln1_g, ln1_b, w_router, router_bias, w_exp_gate, w_exp_up, w_exp_down,
           w_sh_gate, w_sh_up, w_sh_down, ln2_g, ln2_b, w_ple_gate, b_ple_gate, w_ple_proj, ln3_g, ln3_b):
    depth = w_in.shape[0]
    batch, seq, d = x.shape
    t = batch * seq
    n_e = w_router.shape[2]
    alpha = (2.0 * depth) ** 0.25
    row = lambda a: a.reshape(1, -1)

    tm_e = _tile(t * TOP_K // n_e, 256, V7X_SUBLANES * 2)
    n_grid = t * TOP_K // tm_e + n_e
    tm_s = _tile(t, 256, V7X_SUBLANES * 2)
    n_grid_s = t // tm_s + 1
    spare_s = t + jnp.arange(tm_s, dtype=jnp.int32)
    dst_shared = jnp.concatenate([spare_s, jnp.arange(t, dtype=jnp.int32), spare_s])
    shared_tiles = jnp.zeros((n_grid_s,), jnp.int32)
    shared_used = jnp.full((1,), t // tm_s, jnp.int32)

    wg16, wu16, wd16 = w_exp_gate.astype(BF16), w_exp_up.astype(BF16), w_exp_down.astype(BF16)
    sg16, su16, sd16 = (w.astype(BF16)[:, None] for w in (w_sh_gate, w_sh_up, w_sh_down))

    h32 = x.reshape(t, d)
    h16 = h32.astype(BF16)
    p2 = p.reshape(depth, t, p.shape[-1])
    for i in range(depth):
        z = matmul(h16, w_in[i].astype(BF16), BF16)
        ycat = mixer(z, batch, seq, w_pool[i].astype(BF16), row(pool_scale[i]), row(gmlp_ln_g[i]),
                     row(gmlp_ln_b[i]), w_spatial[i], b_spatial[i].T, w_conv[i], row(norm_g_gmlp[i]),
                     row(norm_g_conv[i]))
        h32, _ = proj_ln(ycat, w_out[i].astype(BF16), h32, row(ln1_g[i]), row(ln1_b[i]), alpha)
        wr_t = w_router[i].T
        wr_hi = wr_t.astype(BF16)
        wr_lo = (wr_t - wr_hi.astype(F32)).astype(BF16)
        eidx_t, wts_t, rank_t, counts = router(h32, wr_hi, wr_lo, router_bias[i].reshape(n_e, 1))
        dst_all, tile_expert, n_used = _dispatch_plan(eidx_t, rank_t, counts[:, 0], tm_e, n_grid)
        y_slots = moe_grouped(h32, dst_all, tile_expert, n_used, wg16, wu16, wd16, i, tm_e, TOP_K * t + tm_e)
        y_shared = moe_grouped(h32, dst_shared, shared_tiles, shared_used, sg16, su16, sd16, i, tm_s, t + tm_s)
        h32, h16 = combine_ln(h32, y_shared, y_slots, wts_t.T, row(ln2_g[i]), row(ln2_b[i]), alpha)
        h32, h16 = proj_ln(h16, w_ple_gate[i].astype(BF16), h32, row(ln3_g[i]), row(ln3_b[i]), alpha,
                           ple=(row(b_ple_gate[i]), p2[i], w_ple_proj[i].astype(BF16)))
    return h32.reshape(batch, seq, d)
```

```python
import functools
import math

import jax
import jax.numpy as jnp
from jax import lax
from jax.experimental import pallas as pl
from jax.experimental.pallas import tpu as pltpu

F32 = jnp.float32
BF16 = jnp.bfloat16

POOL_WINDOWS = (2, 4, 8, 16)
TOP_K = 8
N_GROUPS = 8
TOPK_GROUPS = 4
ROUTED_SCALE = 2.5
LN_EPS = 1e-5
RMS_EPS = 1e-6

V7X_LANES = 128
V7X_SUBLANES = 8
V7X_VMEM_LIMIT_BYTES = 60 * 1024 * 1024
MAX_HALO = 16
MOE_ROW_GROUPS = 2
def _tile(n, pref, align):
    t = min(pref, n)
    t -= t % align
    while t >= align:
        if n % t == 0:
            return t
        t -= align
    return n


def _params(sem, flags=None):
    return pltpu.CompilerParams(dimension_semantics=sem, vmem_limit_bytes=V7X_VMEM_LIMIT_BYTES, flags=flags)


def _sigmoid(x):
    return 1.0 / (1.0 + jnp.exp(-x))


def _matmul_kernel(x_ref, w_ref, o_ref):
    o_ref[...] = jnp.dot(x_ref[...], w_ref[...], preferred_element_type=F32).astype(o_ref.dtype)


def matmul(x, w, out_dtype):
    m, k = x.shape
    n = w.shape[1]
    tm = _tile(m, 1024, V7X_SUBLANES * 2)
    tn = _tile(n, 512, V7X_LANES)
    return pl.pallas_call(
        _matmul_kernel,
        grid=(m // tm, n // tn),
        in_specs=[pl.BlockSpec((tm, k), lambda i, j: (i, 0)),
                  pl.BlockSpec((k, tn), lambda i, j: (0, j))],
        out_specs=pl.BlockSpec((tm, tn), lambda i, j: (i, j)),
        out_shape=jax.ShapeDtypeStruct((m, n), out_dtype),
        compiler_params=_params(("parallel", "arbitrary")),
        name="in_proj",
    )(x, w)


def _mixer_kernel(z_ref, wpool_ref, pscale_ref, lng_ref, lnb_ref, ws_ref, bs_ref, wconv_ref,
                  ngg_ref, ngc_ref, o_ref, pool_carry, conv_carry, gm_buf, *, dims):
    pool_w, gdim, gm_w, chunk, conv_w = dims
    ts = z_ref.shape[0]
    s = pl.program_id(1)
    n_groups = pool_w // gdim
    heads = gm_w // chunk

    @pl.when(s == 0)
    def _():
        pool_carry[...] = jnp.zeros_like(pool_carry)
        conv_carry[...] = jnp.zeros_like(conv_carry)

    pos = (s * ts + lax.broadcasted_iota(jnp.int32, (ts, 1), 0)).astype(F32)

    a = z_ref[:, 0:pool_w].astype(F32)
    ext = jnp.concatenate([pool_carry[...], a], axis=0)
    pool_carry[...] = a[ts - MAX_HALO:, :]
    ys = []
    ssq = jnp.zeros((ts, 1), F32)
    for g in range(n_groups):
        win = POOL_WINDOWS[g]
        acc = ext[:, g * gdim:(g + 1) * gdim]
        span = 1
        while span < win:
            acc = acc + pltpu.roll(acc, span, axis=0)
            span *= 2
        count = jnp.minimum(pos + 1.0, float(win))
        d = acc[MAX_HALO:, :] / count - a[:, g * gdim:(g + 1) * gdim]
        y = jnp.dot(d.astype(BF16), wpool_ref[g], preferred_element_type=F32)
        ssq = ssq + jnp.sum(y * y, axis=-1, keepdims=True)
        ys.append(y)
    rinv = lax.rsqrt(ssq / pool_w + RMS_EPS)
    for g in range(n_groups):
        o_ref[:, g * gdim:(g + 1) * gdim] = (
            ys[g] * rinv * pscale_ref[:, g * gdim:(g + 1) * gdim]).astype(o_ref.dtype)

    uv = jax.nn.gelu(z_ref[:, pool_w:pool_w + 2 * gm_w].astype(F32))
    u = uv[:, :gm_w]
    v = uv[:, gm_w:]
    mu = jnp.mean(v, axis=-1, keepdims=True)
    vc = v - mu
    var = jnp.mean(vc * vc, axis=-1, keepdims=True)
    vn = (vc * lax.rsqrt(var + LN_EPS) * lng_ref[...] + lnb_ref[...]).astype(BF16)
    row = lax.broadcasted_iota(jnp.int32, (chunk, chunk), 0)
    col = lax.broadcasted_iota(jnp.int32, (chunk, chunk), 1)
    ssq = jnp.zeros((ts, 1), F32)
    for h in range(heads):
        wsm = jnp.where(row >= col, ws_ref[h], 0.0).astype(BF16)
        bias = bs_ref[:, h:h + 1]
        parts = []
        for c in range(ts // chunk):
            blk = vn[c * chunk:(c + 1) * chunk, h * chunk:(h + 1) * chunk]
            parts.append(jnp.dot(wsm, blk, preferred_element_type=F32) + bias)
        sv = jnp.concatenate(parts, axis=0) if len(parts) > 1 else parts[0]
        y = u[:, h * chunk:(h + 1) * chunk] * sv
        ssq = ssq + jnp.sum(y * y, axis=-1, keepdims=True)
        gm_buf[:, h * chunk:(h + 1) * chunk] = y
    rinv = lax.rsqrt(ssq / gm_w + RMS_EPS)
    o_ref[:, pool_w:pool_w + gm_w] = (gm_buf[...] * rinv * ngg_ref[...]).astype(o_ref.dtype)

    off = pool_w + 2 * gm_w
    bg = z_ref[:, off:off + conv_w].astype(F32)
    x = z_ref[:, off + conv_w:off + 2 * conv_w].astype(F32) * z_ref[:, off + 2 * conv_w:off + 3 * conv_w].astype(F32)
    ext = jnp.concatenate([conv_carry[...], x], axis=0)
    conv_carry[...] = x[ts - V7X_SUBLANES:, :]
    zc = (wconv_ref[0:1, :] * pltpu.roll(ext, 2, axis=0)[V7X_SUBLANES:, :]
          + wconv_ref[1:2, :] * pltpu.roll(ext, 1, axis=0)[V7X_SUBLANES:, :]
          + wconv_ref[2:3, :] * x)
    y = bg * zc
    rinv = lax.rsqrt(jnp.mean(y * y, axis=-1, keepdims=True) + RMS_EPS)
    o_ref[:, pool_w + gm_w:pool_w + gm_w + conv_w] = (y * rinv * ngc_ref[...]).astype(o_ref.dtype)


def mixer(z, batch, seq, w_pool, pool_scale, ln_g, ln_b, w_spatial, b_spatial_t, w_conv, ng_gmlp, ng_conv):
    n_groups, gdim, _ = w_pool.shape
    pool_w = n_groups * gdim
    heads, chunk, _ = w_spatial.shape
    gm_w = heads * chunk
    conv_w = w_conv.shape[1]
    assert w_conv.shape[0] == 3 and n_groups == len(POOL_WINDOWS)
    width = z.shape[1]
    assert width == pool_w + 2 * gm_w + 3 * conv_w
    mix_w = pool_w + gm_w + conv_w
    ts = _tile(seq, 256, chunk)
    n_s = seq // ts
    full = lambda a: pl.BlockSpec(a.shape, lambda b, s: (0,) * a.ndim)
    kern = functools.partial(_mixer_kernel, dims=(pool_w, gdim, gm_w, chunk, conv_w))
    return pl.pallas_call(
        kern,
        grid=(batch, n_s),
        in_specs=[pl.BlockSpec((ts, width), lambda b, s: (b * n_s + s, 0)),
                  full(w_pool), full(pool_scale), full(ln_g), full(ln_b), full(w_spatial),
                  full(b_spatial_t), full(w_conv), full(ng_gmlp), full(ng_conv)],
        out_specs=pl.BlockSpec((ts, mix_w), lambda b, s: (b * n_s + s, 0)),
        out_shape=jax.ShapeDtypeStruct((batch * seq, mix_w), BF16),
        scratch_shapes=[pltpu.VMEM((MAX_HALO, pool_w), F32),
                        pltpu.VMEM((V7X_SUBLANES, conv_w), F32),
                        pltpu.VMEM((ts, gm_w), F32)],
        compiler_params=_params(("arbitrary", "arbitrary")),
        name="mixer",
    )(z, w_pool, pool_scale, ln_g, ln_b, w_spatial, b_spatial_t, w_conv, ng_gmlp, ng_conv)


def _layer_norm_store(acc, n_j, tn, n, g_ref, b_ref, o32_ref, o16_ref=None):
    tot = jnp.sum(acc[0], axis=-1, keepdims=True)
    for jj in range(1, n_j):
        tot = tot + jnp.sum(acc[jj], axis=-1, keepdims=True)
    mu = tot / n
    c0 = acc[0] - mu
    sq = jnp.sum(c0 * c0, axis=-1, keepdims=True)
    for jj in range(1, n_j):
        c = acc[jj] - mu
        sq = sq + jnp.sum(c * c, axis=-1, keepdims=True)
    rstd = lax.rsqrt(sq / n + LN_EPS)

    for jj in range(n_j):
        sl = slice(jj * tn, (jj + 1) * tn)
        y = (acc[jj] - mu) * rstd * g_ref[:, sl] + b_ref[:, sl]
        o32_ref[:, sl] = y
        if o16_ref is not None:
            o16_ref[:, sl] = y.astype(BF16)


def _proj_ln_kernel(x_ref, w_ref, res_ref, g_ref, b_ref, o32_ref, acc, *, alpha):
    j = pl.program_id(1)
    n_j, _, tn = acc.shape
    y = jnp.dot(x_ref[...], w_ref[...], preferred_element_type=F32)
    acc[j] = alpha * res_ref[...] + y

    @pl.when(j == n_j - 1)
    def _():
        _layer_norm_store(acc, n_j, tn, n_j * tn, g_ref, b_ref, o32_ref)


def _ple_ln_kernel(x_ref, w_ref, bias_ref, p_ref, wp_ref, res_ref, g_ref, b_ref, o32_ref, o16_ref, acc,
                   *, alpha):
    j = pl.program_id(1)
    n_j, _, tn = acc.shape
    gate = _sigmoid(jnp.dot(x_ref[...], w_ref[...], preferred_element_type=F32) + bias_ref[...])
    proj = jnp.dot(p_ref[...].astype(BF16), wp_ref[...], preferred_element_type=F32)
    acc[j] = alpha * res_ref[...] + gate * proj

    @pl.when(j == n_j - 1)
    def _():
        _layer_norm_store(acc, n_j, tn, n_j * tn, g_ref, b_ref, o32_ref, o16_ref)


def proj_ln(x, w, res, ln_g, ln_b, alpha, ple=None):
    m, k = x.shape
    n = w.shape[1]
    tm = _tile(m, 512, V7X_SUBLANES * 2)
    tn = _tile(n, 512, V7X_LANES)
    n_j = n // tn
    x_spec = pl.BlockSpec((tm, k), lambda i, j: (i, 0))
    w_spec = pl.BlockSpec((k, tn), lambda i, j: (0, j))
    col_spec = pl.BlockSpec((tm, tn), lambda i, j: (i, j))
    vec_full = pl.BlockSpec((1, n), lambda i, j: (0, 0))
    row_spec = pl.BlockSpec((tm, n), lambda i, j: (i, 0), pipeline_mode=pl.Buffered(1))
    if ple is None:
        kern = functools.partial(_proj_ln_kernel, alpha=alpha)
        in_specs = [x_spec, w_spec, col_spec, vec_full, vec_full]
        args = (x, w, res, ln_g, ln_b)
        name = "out_proj_ln"
    else:
        bias, p, w_p = ple
        kp = p.shape[1]
        kern = functools.partial(_ple_ln_kernel, alpha=alpha)
        in_specs = [x_spec, w_spec, pl.BlockSpec((1, tn), lambda i, j: (0, j)),
                    pl.BlockSpec((tm, kp), lambda i, j: (i, 0)),
                    pl.BlockSpec((kp, tn), lambda i, j: (0, j)),
                    col_spec, vec_full, vec_full]
        args = (x, w, bias, p, w_p, res, ln_g, ln_b)
        name = "ple_ln"
    n_out = 1 if ple is None else 2
    return pl.pallas_call(
        kern,
        grid=(m // tm, n_j),
        in_specs=in_specs,
        out_specs=[row_spec, row_spec][:n_out],
        out_shape=[jax.ShapeDtypeStruct((m, n), F32), jax.ShapeDtypeStruct((m, n), BF16)][:n_out],
        scratch_shapes=[pltpu.VMEM((n_j, tm, tn), F32)],
        compiler_params=_params(("parallel", "arbitrary")),
        name=name,
    )(*args)


def _first_argmax(x, idx, axis, big):
    m = jnp.max(x, axis=axis, keepdims=True)
    first = jnp.min(jnp.where(x == m, idx, big), axis=axis, keepdims=True)
    return m, first


def _router_kernel(x_ref, wh_ref, wl_ref, bias_ref, eidx_ref, wts_ref, rank_ref, cnt_ref, carry):
    tm = x_ref.shape[0]
    n_e = wh_ref.shape[0]
    per = n_e // N_GROUPS
    i = pl.program_id(0)

    @pl.when(i == 0)
    def _():
        carry[...] = jnp.zeros_like(carry)

    x = x_ref[...]
    xh = x.astype(BF16)
    xl = (x - xh.astype(F32)).astype(BF16)
    nt = (((1,), (1,)), ((), ()))
    logits = (lax.dot_general(wh_ref[...], xh, nt, preferred_element_type=F32)
              + lax.dot_general(wh_ref[...], xl, nt, preferred_element_type=F32)
              + lax.dot_general(wl_ref[...], xh, nt, preferred_element_type=F32))
    scores = _sigmoid(logits)
    sel = scores + bias_ref[...]

    sel3 = sel.reshape(N_GROUPS, per, tm)
    j_idx = lax.broadcasted_iota(jnp.int32, (N_GROUPS, per, tm), 1).astype(F32)
    m1, a1 = _first_argmax(sel3, j_idx, 1, float(per))
    m2 = jnp.max(jnp.where(j_idx == a1, -jnp.inf, sel3), axis=1, keepdims=True)
    gscore = (m1 + m2).reshape(N_GROUPS, tm)
    g_idx = lax.broadcasted_iota(jnp.int32, (N_GROUPS, tm), 0).astype(F32)
    gmask = jnp.zeros((N_GROUPS, tm), F32)
    work = gscore
    for _ in range(TOPK_GROUPS):
        _, ga = _first_argmax(work, g_idx, 0, float(N_GROUPS))
        hit = g_idx == ga
        gmask = jnp.where(hit, 1.0, gmask)
        work = jnp.where(hit, -jnp.inf, work)
    e_idx = lax.broadcasted_iota(jnp.int32, (n_e, tm), 0).astype(F32)
    emask = jnp.broadcast_to(gmask.reshape(N_GROUPS, 1, tm), (N_GROUPS, per, tm)).reshape(n_e, tm)
    work = jnp.where(emask > 0.5, sel, -jnp.inf)

    e_rows, s_rows, hots = [], [], []
    for _ in range(TOP_K):
        _, ea = _first_argmax(work, e_idx, 0, float(n_e))
        hit = e_idx == ea
        work = jnp.where(hit, -jnp.inf, work)
        e_rows.append(ea)
        s_rows.append(jnp.sum(jnp.where(hit, scores, 0.0), axis=0, keepdims=True))
        hots.append(jnp.where(hit, 1.0, 0.0))
    e_sel = jnp.concatenate(e_rows, axis=0)
    s_sel = jnp.concatenate(s_rows, axis=0)
    eidx_ref[...] = e_sel.astype(jnp.int32)
    wts_ref[...] = ROUTED_SCALE * s_sel / jnp.sum(s_sel, axis=0, keepdims=True)

    hot = jnp.concatenate(hots, axis=0)
    r_i = lax.broadcasted_iota(jnp.int32, (tm, tm), 0)
    c_i = lax.broadcasted_iota(jnp.int32, (tm, tm), 1)
    upper = jnp.where(r_i < c_i, 1.0, 0.0).astype(BF16)
    prefix = jnp.dot(hot.astype(BF16), upper, preferred_element_type=F32)
    base = carry[...]
    ranks = []
    for k in range(TOP_K):
        hk = hots[k]
        pk = prefix[k * n_e:(k + 1) * n_e, :] + base
        ranks.append(jnp.sum(hk * pk, axis=0, keepdims=True))
        base = base + jnp.sum(hk, axis=1, keepdims=True)
    rank_ref[...] = jnp.concatenate(ranks, axis=0).astype(jnp.int32)
    carry[...] = base
    cnt_ref[...] = base.astype(jnp.int32)


def router(h, w_hi_t, w_lo_t, bias_col):
    t, d = h.shape
    n_e = w_hi_t.shape[0]
    tm = _tile(t, 256, V7X_LANES)
    full = lambda a: pl.BlockSpec(a.shape, lambda i: (0,) * a.ndim)
    tok_spec = pl.BlockSpec((TOP_K, tm), lambda i: (0, i))
    return pl.pallas_call(
        _router_kernel,
        grid=(t // tm,),
        in_specs=[pl.BlockSpec((tm, d), lambda i: (i, 0)), full(w_hi_t), full(w_lo_t), full(bias_col)],
        out_specs=[tok_spec, tok_spec, tok_spec, pl.BlockSpec((n_e, 1), lambda i: (0, 0))],
        out_shape=[jax.ShapeDtypeStruct((TOP_K, t), jnp.int32),
                   jax.ShapeDtypeStruct((TOP_K, t), F32),
                   jax.ShapeDtypeStruct((TOP_K, t), jnp.int32),
                   jax.ShapeDtypeStruct((n_e, 1), jnp.int32)],
        scratch_shapes=[pltpu.VMEM((n_e, 1), F32)],
        compiler_params=_params(("arbitrary",)),
        name="router",
    )(h, w_hi_t, w_lo_t, bias_col)


def _moe_kernel(te_ref, nu_ref, tok_cur, tok_next, dst_prev, x_hbm, wg_ref, wu_ref, wd_ref, y_hbm,
                xbuf, ybuf, wgu, wdn, gsem, ssem):
    i = pl.program_id(0)
    n_used = nu_ref[0]
    tm = xbuf.shape[1]
    f = wd_ref.shape[0]
    slot = i % 2
    other = 1 - slot

    def gather_row(idx_ref, r, s):
        pltpu.make_async_copy(x_hbm.at[pl.ds(idx_ref[0, 0, r], 1)], xbuf.at[s, pl.ds(r, 1)], gsem.at[s]).start()

    def scatter_row(r, s):
        pltpu.make_async_copy(ybuf.at[s, pl.ds(r, 1)], y_hbm.at[pl.ds(dst_prev[0, 0, r], 1)], ssem.at[s]).start()

    def wait_gather(s):
        pltpu.make_async_copy(x_hbm.at[pl.ds(0, tm)], xbuf.at[s], gsem.at[s]).wait()

    def wait_scatter(s):
        pltpu.make_async_copy(ybuf.at[s], y_hbm.at[pl.ds(0, tm)], ssem.at[s]).wait()

    def loop_rows(fn):
        def body(r, carry):
            fn(r)
            return carry
        lax.fori_loop(0, tm, body, 0)

    @pl.when(i == 0)
    def _():
        ybuf[1] = jnp.zeros(ybuf.shape[1:], F32)
        loop_rows(lambda r: gather_row(tok_cur, r, 0))

    for parity in range(2):
        @pl.when(jnp.logical_and(i < n_used, slot == parity))
        def _():
            for r in range(tm):
                gather_row(tok_next, r, 1 - parity)
                scatter_row(r, 1 - parity)

    @pl.when(jnp.logical_and(i >= 1, i <= n_used))
    def _():
        wait_scatter(slot)

    @pl.when(jnp.logical_and(i < n_used, jnp.logical_or(i == 0, te_ref[i] != te_ref[jnp.maximum(i - 1, 0)])))
    def _():
        def fuse(k0, carry):
            rows = pl.ds(pl.multiple_of(k0 * 256, 256), 256)
            wgu[rows, 0:f] = wg_ref[rows, :]
            wgu[rows, f:2 * f] = wu_ref[rows, :]
            return carry
        lax.fori_loop(0, wgu.shape[0] // 256, fuse, 0)

        def cast(k0, carry):
            rows = pl.ds(pl.multiple_of(k0 * 16, 16), 16)
            wdn[rows, :] = wd_ref[rows, :].astype(BF16)
            return carry
        lax.fori_loop(0, f // 16, cast, 0)

    @pl.when(i < n_used)
    def _():
        wait_gather(slot)
        rows = tm // MOE_ROW_GROUPS
        for h in range(MOE_ROW_GROUPS):
            rs = slice(h * rows, (h + 1) * rows)
            gu = jnp.dot(xbuf[slot, rs, :].astype(BF16), wgu[...], preferred_element_type=F32)
            g = gu[:, 0:f]
            a = (g * _sigmoid(g) * gu[:, f:2 * f]).astype(BF16)
            ybuf[slot, rs, :] = jnp.dot(a, wdn[...], preferred_element_type=F32)

    @pl.when(i == n_used)
    def _():
        wait_gather(slot)
        loop_rows(lambda r: scatter_row(r, other))
        wait_scatter(other)


def moe_grouped(x, dst_all, tile_expert, n_used, wg, wu, wd, layer, tm, out_rows):
    n_grid = tile_expert.shape[0]
    t, d = x.shape
    f = wd.shape[2]
    dst3 = dst_all.reshape(n_grid + 1, 1, tm)
    tok3 = dst3 % t
    smem_blk = lambda fn: pl.BlockSpec((1, 1, tm), fn, memory_space=pltpu.SMEM)
    w_up_spec = pl.BlockSpec((None, None, d, f), lambda i, te, nu: (layer, te[i], 0, 0))
    grid_spec = pltpu.PrefetchScalarGridSpec(
        num_scalar_prefetch=2,
        grid=(n_grid,),
        in_specs=[smem_blk(lambda i, te, nu: (i + 1, 0, 0)),
                  smem_blk(lambda i, te, nu: (jnp.minimum(i + 2, n_grid), 0, 0)),
                  smem_blk(lambda i, te, nu: (i, 0, 0)),
                  pl.BlockSpec(memory_space=pl.ANY),
                  w_up_spec, w_up_spec,
                  pl.BlockSpec((None, None, f, d), lambda i, te, nu: (layer, te[i], 0, 0))],
        out_specs=pl.BlockSpec(memory_space=pl.ANY),
        scratch_shapes=[pltpu.VMEM((2, tm, d), F32), pltpu.VMEM((2, tm, d), F32),
                        pltpu.VMEM((d, 2 * f), BF16), pltpu.VMEM((f, d), BF16),
                        pltpu.SemaphoreType.DMA((2,)), pltpu.SemaphoreType.DMA((2,))],
    )
    return pl.pallas_call(
        _moe_kernel,
        grid_spec=grid_spec,
        out_shape=jax.ShapeDtypeStruct((out_rows, d), F32),
        compiler_params=_params(("arbitrary",)),
        name="moe_grouped",
    )(tile_expert, n_used, tok3, tok3, dst3, x, wg, wu, wd)


def _shared_kernel(x_ref, wgu_ref, wd_ref, o_ref):
    f = wd_ref.shape[0]
    rows = x_ref.shape[0] // MOE_ROW_GROUPS
    for h in range(MOE_ROW_GROUPS):
        rs = slice(h * rows, (h + 1) * rows)
        gu = jnp.dot(x_ref[rs, :].astype(BF16), wgu_ref[...], preferred_element_type=F32)
        g = gu[:, 0:f]
        a = (g * _sigmoid(g) * gu[:, f:2 * f]).astype(BF16)
        o_ref[rs, :] = jnp.dot(a, wd_ref[...], preferred_element_type=F32)


def shared_swiglu(x, wgu, wd, layer):
    t, d = x.shape
    f = wd.shape[1]
    tm = _tile(t, 256, V7X_SUBLANES * 2)
    return pl.pallas_call(
        _shared_kernel,
        grid=(t // tm,),
        in_specs=[pl.BlockSpec((tm, d), lambda i: (i, 0)),
                  pl.BlockSpec((None, d, 2 * f), lambda i: (layer, 0, 0)),
                  pl.BlockSpec((None, f, d), lambda i: (layer, 0, 0))],
        out_specs=pl.BlockSpec((tm, d), lambda i: (i, 0)),
        out_shape=jax.ShapeDtypeStruct((t, d), F32),
        compiler_params=_params(("parallel",)),
        name="shared_swiglu",
    )(x, wgu, wd)


def _combine_kernel(wts_ref, h_ref, sh_ref, *rest, alpha):
    planes = rest[:TOP_K]
    g_ref, b_ref, o32_ref, o16_ref = rest[TOP_K:]
    acc = alpha * h_ref[...] + sh_ref[...]
    for k in range(TOP_K):
        acc = acc + wts_ref[:, k:k + 1] * planes[k][...]
    mu = jnp.mean(acc, axis=-1, keepdims=True)
    c = acc - mu
    var = jnp.mean(c * c, axis=-1, keepdims=True)
    y = c * lax.rsqrt(var + LN_EPS) * g_ref[...] + b_ref[...]
    o32_ref[...] = y
    o16_ref[...] = y.astype(BF16)


def combine_ln(h, y_shared, y_slots, wts, ln_g, ln_b, alpha):
    t, d = h.shape
    tm = _tile(t, 128, V7X_SUBLANES * 2)
    n_i = t // tm
    row = pl.BlockSpec((tm, d), lambda i: (i, 0))
    vec = pl.BlockSpec((1, d), lambda i: (0, 0))
    plane = lambda k: pl.BlockSpec((tm, d), lambda i: (k * n_i + i, 0))
    return pl.pallas_call(
        functools.partial(_combine_kernel, alpha=alpha),
        grid=(n_i,),
        in_specs=[pl.BlockSpec((tm, TOP_K), lambda i: (i, 0)), row, plane(0)]
        + [plane(k) for k in range(TOP_K)] + [vec, vec],
        out_specs=[row, row],
        out_shape=[jax.ShapeDtypeStruct((t, d), F32), jax.ShapeDtypeStruct((t, d), BF16)],
        compiler_params=_params(("parallel",)),
        name="combine_ln",
    )(wts, h, y_shared, *([y_slots] * TOP_K), ln_g, ln_b)


def _dispatch_plan(eidx_t, rank_t, counts, tm, n_grid):
    t = eidx_t.shape[1]
    n_e = counts.shape[0]
    padded = ((counts + tm - 1) // tm) * tm
    ends = jnp.cumsum(padded)
    starts = ends - padded
    hit = eidx_t[:, :, None] == jnp.arange(n_e, dtype=jnp.int32)
    pos = jnp.sum(jnp.where(hit, starts, 0), axis=-1) + rank_t
    slot = (jnp.arange(TOP_K, dtype=jnp.int32)[:, None] * t + jnp.arange(t, dtype=jnp.int32)[None, :])
    spare = TOP_K * t + jnp.arange((n_grid + 1) * tm, dtype=jnp.int32) % tm
    dst_all = spare.at[(tm + pos).reshape(-1)].set(slot.reshape(-1), unique_indices=True,
                                                   mode="promise_in_bounds")
    tile_start = jnp.arange(n_grid, dtype=jnp.int32) * tm
    tile_expert = jnp.minimum(jnp.sum(ends[None, :] <= tile_start[:, None], axis=1), n_e - 1).astype(jnp.int32)
    n_used = (ends[-1] // tm).astype(jnp.int32).reshape(1)
    return dst_all, tile_expert, n_used


def kernel(x, p, w_in, w_pool, pool_scale, gmlp_ln_g, gmlp_ln_b, w_spatial, b_spatial, w_conv, norm_g_gmlp,
           norm_g_conv, w_out, ln1_g, ln1_b, w_router, router_bias, w_exp_gate, w_exp_up, w_exp_down,
           w_sh_gate, w_sh_up, w_sh_down, ln2_g, ln2_b, w_ple_gate, b_ple_gate, w_ple_proj, ln3_g, ln3_b):
    depth = w_in.shape[0]
    batch, seq, d = x.shape
    t = batch * seq
    n_e = w_router.shape[2]
    alpha = (2.0 * depth) ** 0.25
    row = lambda a: a.reshape(1, -1)

    tm_e = _tile(t * TOP_K // n_e, 256, V7X_SUBLANES * 2)
    n_grid = t * TOP_K // tm_e + n_e

    wg16, wu16 = w_exp_gate.astype(BF16), w_exp_up.astype(BF16)
    sgu16 = jnp.concatenate([w_sh_gate, w_sh_up], axis=-1).astype(BF16)
    sd16 = w_sh_down.astype(BF16)

    h32 = x.reshape(t, d)
    h16 = h32.astype(BF16)
    p2 = p.reshape(depth, t, p.shape[-1])
    for i in range(depth):
        z = matmul(h16, w_in[i].astype(BF16), BF16)
        ycat = mixer(z, batch, seq, w_pool[i].astype(BF16), row(pool_scale[i]), row(gmlp_ln_g[i]),
                     row(gmlp_ln_b[i]), w_spatial[i], b_spatial[i].T, w_conv[i], row(norm_g_gmlp[i]),
                     row(norm_g_conv[i]))
        h32, = proj_ln(ycat, w_out[i].astype(BF16), h32, row(ln1_g[i]), row(ln1_b[i]), alpha)
        wr_t = w_router[i].T
        wr_hi = wr_t.astype(BF16)
        wr_lo = (wr_t - wr_hi.astype(F32)).astype(BF16)
        eidx_t, wts_t, rank_t, counts = router(h32, wr_hi, wr_lo, router_bias[i].reshape(n_e, 1))
        dst_all, tile_expert, n_used = _dispatch_plan(eidx_t, rank_t, counts[:, 0], tm_e, n_grid)
        y_slots = moe_grouped(h32, dst_all, tile_expert, n_used, wg16, wu16, w_exp_down, i, tm_e,
                              TOP_K * t + tm_e)
        y_shared = shared_swiglu(h32, sgu16, sd16, i)
        h32, h16 = combine_ln(h32, y_shared, y_slots, wts_t.T, row(ln2_g[i]), row(ln2_b[i]), alpha)
        h32, h16 = proj_ln(h16, w_ple_gate[i].astype(BF16), h32, row(ln3_g[i]), row(ln3_b[i]), alpha,
                           ple=(row(b_ple_gate[i]), p2[i], w_ple_proj[i].astype(BF16)))
    return h32.reshape(batch, seq, d)
```

```python
import functools
import math

import jax
import jax.numpy as jnp
from jax import lax
from jax.experimental import pallas as pl
from jax.experimental.pallas import tpu as pltpu

F32 = jnp.float32
BF16 = jnp.bfloat16

POOL_WINDOWS = (2, 4, 8, 16)
TOP_K = 8
N_GROUPS = 8
TOPK_GROUPS = 4
ROUTED_SCALE = 2.5
LN_EPS = 1e-5
RMS_EPS = 1e-6

V7X_LANES = 128
V7X_SUBLANES = 8
V7X_VMEM_LIMIT_BYTES = 60 * 1024 * 1024
MAX_HALO = 16
MOE_ROW_GROUPS = 2
def _tile(n, pref, align):
    t = min(pref, n)
    t -= t % align
    while t >= align:
        if n % t == 0:
            return t
        t -= align
    return n


def _params(sem, flags=None):
    return pltpu.CompilerParams(dimension_semantics=sem, vmem_limit_bytes=V7X_VMEM_LIMIT_BYTES, flags=flags)


def _sigmoid(x):
    return 1.0 / (1.0 + jnp.exp(-x))


def _matmul_kernel(x_ref, w_ref, o_ref):
    o_ref[...] = jnp.dot(x_ref[...], w_ref[...], preferred_element_type=F32).astype(o_ref.dtype)


def matmul(x, w, out_dtype):
    m, k = x.shape
    n = w.shape[1]
    tm = _tile(m, 1024, V7X_SUBLANES * 2)
    tn = _tile(n, 512, V7X_LANES)
    return pl.pallas_call(
        _matmul_kernel,
        grid=(m // tm, n // tn),
        in_specs=[pl.BlockSpec((tm, k), lambda i, j: (i, 0)),
                  pl.BlockSpec((k, tn), lambda i, j: (0, j))],
        out_specs=pl.BlockSpec((tm, tn), lambda i, j: (i, j)),
        out_shape=jax.ShapeDtypeStruct((m, n), out_dtype),
        compiler_params=_params(("parallel", "arbitrary")),
        name="in_proj",
    )(x, w)


def _mixer_kernel(z_ref, wpool_ref, pscale_ref, lng_ref, lnb_ref, ws_ref, bs_ref, wconv_ref,
                  ngg_ref, ngc_ref, o_ref, pool_carry, conv_carry, gm_buf, *, dims):
    pool_w, gdim, gm_w, chunk, conv_w = dims
    ts = z_ref.shape[0]
    s = pl.program_id(1)
    n_groups = pool_w // gdim
    heads = gm_w // chunk

    @pl.when(s == 0)
    def _():
        pool_carry[...] = jnp.zeros_like(pool_carry)
        conv_carry[...] = jnp.zeros_like(conv_carry)

    pos = (s * ts + lax.broadcasted_iota(jnp.int32, (ts, 1), 0)).astype(F32)

    a = z_ref[:, 0:pool_w].astype(F32)
    ext = jnp.concatenate([pool_carry[...], a], axis=0)
    pool_carry[...] = a[ts - MAX_HALO:, :]
    ys = []
    ssq = jnp.zeros((ts, 1), F32)
    for g in range(n_groups):
        win = POOL_WINDOWS[g]
        acc = ext[:, g * gdim:(g + 1) * gdim]
        span = 1
        while span < win:
            acc = acc + pltpu.roll(acc, span, axis=0)
            span *= 2
        count = jnp.minimum(pos + 1.0, float(win))
        d = acc[MAX_HALO:, :] / count - a[:, g * gdim:(g + 1) * gdim]
        y = jnp.dot(d.astype(BF16), wpool_ref[g], preferred_element_type=F32)
        ssq = ssq + jnp.sum(y * y, axis=-1, keepdims=True)
        ys.append(y)
    rinv = lax.rsqrt(ssq / pool_w + RMS_EPS)
    for g in range(n_groups):
        o_ref[:, g * gdim:(g + 1) * gdim] = (
            ys[g] * rinv * pscale_ref[:, g * gdim:(g + 1) * gdim]).astype(o_ref.dtype)

    uv = jax.nn.gelu(z_ref[:, pool_w:pool_w + 2 * gm_w].astype(F32))
    u = uv[:, :gm_w]
    v = uv[:, gm_w:]
    mu = jnp.mean(v, axis=-1, keepdims=True)
    vc = v - mu
    var = jnp.mean(vc * vc, axis=-1, keepdims=True)
    vn = (vc * lax.rsqrt(var + LN_EPS) * lng_ref[...] + lnb_ref[...]).astype(BF16)
    row = lax.broadcasted_iota(jnp.int32, (chunk, chunk), 0)
    col = lax.broadcasted_iota(jnp.int32, (chunk, chunk), 1)
    ssq = jnp.zeros((ts, 1), F32)
    for h in range(heads):
        wsm = jnp.where(row >= col, ws_ref[h], 0.0).astype(BF16)
        bias = bs_ref[:, h:h + 1]
        parts = []
        for c in range(ts // chunk):
            blk = vn[c * chunk:(c + 1) * chunk, h * chunk:(h + 1) * chunk]
            parts.append(jnp.dot(wsm, blk, preferred_element_type=F32) + bias)
        sv = jnp.concatenate(parts, axis=0) if len(parts) > 1 else parts[0]
        y = u[:, h * chunk:(h + 1) * chunk] * sv
        ssq = ssq + jnp.sum(y * y, axis=-1, keepdims=True)
        gm_buf[:, h * chunk:(h + 1) * chunk] = y
    rinv = lax.rsqrt(ssq / gm_w + RMS_EPS)
    o_ref[:, pool_w:pool_w + gm_w] = (gm_buf[...] * rinv * ngg_ref[...]).astype(o_ref.dtype)

    off = pool_w + 2 * gm_w
    bg = z_ref[:, off:off + conv_w].astype(F32)
    x = z_ref[:, off + conv_w:off + 2 * conv_w].astype(F32) * z_ref[:, off + 2 * conv_w:off + 3 * conv_w].astype(F32)
    ext = jnp.concatenate([conv_carry[...], x], axis=0)
    conv_carry[...] = x[ts - V7X_SUBLANES:, :]
    zc = (wconv_ref[0:1, :] * pltpu.roll(ext, 2, axis=0)[V7X_SUBLANES:, :]
          + wconv_ref[1:2, :] * pltpu.roll(ext, 1, axis=0)[V7X_SUBLANES:, :]
          + wconv_ref[2:3, :] * x)
    y = bg * zc
    rinv = lax.rsqrt(jnp.mean(y * y, axis=-1, keepdims=True) + RMS_EPS)
    o_ref[:, pool_w + gm_w:pool_w + gm_w + conv_w] = (y * rinv * ngc_ref[...]).astype(o_ref.dtype)


def mixer(z, batch, seq, w_pool, pool_scale, ln_g, ln_b, w_spatial, b_spatial_t, w_conv, ng_gmlp, ng_conv):
    n_groups, gdim, _ = w_pool.shape
    pool_w = n_groups * gdim
    heads, chunk, _ = w_spatial.shape
    gm_w = heads * chunk
    conv_w = w_conv.shape[1]
    assert w_conv.shape[0] == 3 and n_groups == len(POOL_WINDOWS)
    width = z.shape[1]
    assert width == pool_w + 2 * gm_w + 3 * conv_w
    mix_w = pool_w + gm_w + conv_w
    ts = _tile(seq, 256, chunk)
    n_s = seq // ts
    full = lambda a: pl.BlockSpec(a.shape, lambda b, s: (0,) * a.ndim)
    kern = functools.partial(_mixer_kernel, dims=(pool_w, gdim, gm_w, chunk, conv_w))
    return pl.pallas_call(
        kern,
        grid=(batch, n_s),
        in_specs=[pl.BlockSpec((ts, width), lambda b, s: (b * n_s + s, 0)),
                  full(w_pool), full(pool_scale), full(ln_g), full(ln_b), full(w_spatial),
                  full(b_spatial_t), full(w_conv), full(ng_gmlp), full(ng_conv)],
        out_specs=pl.BlockSpec((ts, mix_w), lambda b, s: (b * n_s + s, 0)),
        out_shape=jax.ShapeDtypeStruct((batch * seq, mix_w), BF16),
        scratch_shapes=[pltpu.VMEM((MAX_HALO, pool_w), F32),
                        pltpu.VMEM((V7X_SUBLANES, conv_w), F32),
                        pltpu.VMEM((ts, gm_w), F32)],
        compiler_params=_params(("arbitrary", "arbitrary")),
        name="mixer",
    )(z, w_pool, pool_scale, ln_g, ln_b, w_spatial, b_spatial_t, w_conv, ng_gmlp, ng_conv)


def _layer_norm_store(acc, n_j, tn, n, g_ref, b_ref, o32_ref, o16_ref=None):
    tot = jnp.sum(acc[0], axis=-1, keepdims=True)
    for jj in range(1, n_j):
        tot = tot + jnp.sum(acc[jj], axis=-1, keepdims=True)
    mu = tot / n
    c0 = acc[0] - mu
    sq = jnp.sum(c0 * c0, axis=-1, keepdims=True)
    for jj in range(1, n_j):
        c = acc[jj] - mu
        sq = sq + jnp.sum(c * c, axis=-1, keepdims=True)
    rstd = lax.rsqrt(sq / n + LN_EPS)

    for jj in range(n_j):
        sl = slice(jj * tn, (jj + 1) * tn)
        y = (acc[jj] - mu) * rstd * g_ref[:, sl] + b_ref[:, sl]
        o32_ref[:, sl] = y
        if o16_ref is not None:
            o16_ref[:, sl] = y.astype(BF16)


def _proj_ln_kernel(x_ref, w_ref, res_ref, g_ref, b_ref, o32_ref, acc, *, alpha):
    j = pl.program_id(1)
    n_j, _, tn = acc.shape
    y = jnp.dot(x_ref[...], w_ref[...], preferred_element_type=F32)
    acc[j] = alpha * res_ref[...] + y

    @pl.when(j == n_j - 1)
    def _():
        _layer_norm_store(acc, n_j, tn, n_j * tn, g_ref, b_ref, o32_ref)


def _ple_ln_kernel(x_ref, w_ref, bias_ref, p_ref, wp_ref, res_ref, g_ref, b_ref, o32_ref, o16_ref, acc,
                   *, alpha):
    j = pl.program_id(1)
    n_j, _, tn = acc.shape
    gate = _sigmoid(jnp.dot(x_ref[...], w_ref[...], preferred_element_type=F32) + bias_ref[...])
    proj = jnp.dot(p_ref[...].astype(BF16), wp_ref[...], preferred_element_type=F32)
    acc[j] = alpha * res_ref[...] + gate * proj

    @pl.when(j == n_j - 1)
    def _():
        _layer_norm_store(acc, n_j, tn, n_j * tn, g_ref, b_ref, o32_ref, o16_ref)


def proj_ln(x, w, res, ln_g, ln_b, alpha, ple=None):
    m, k = x.shape
    n = w.shape[1]
    tm = _tile(m, 512, V7X_SUBLANES * 2)
    tn = _tile(n, 512, V7X_LANES)
    n_j = n // tn
    x_spec = pl.BlockSpec((tm, k), lambda i, j: (i, 0))
    w_spec = pl.BlockSpec((k, tn), lambda i, j: (0, j))
    col_spec = pl.BlockSpec((tm, tn), lambda i, j: (i, j))
    vec_full = pl.BlockSpec((1, n), lambda i, j: (0, 0))
    row_spec = pl.BlockSpec((tm, n), lambda i, j: (i, 0), pipeline_mode=pl.Buffered(1))
    if ple is None:
        kern = functools.partial(_proj_ln_kernel, alpha=alpha)
        in_specs = [x_spec, w_spec, col_spec, vec_full, vec_full]
        args = (x, w, res, ln_g, ln_b)
        name = "out_proj_ln"
    else:
        bias, p, w_p = ple
        kp = p.shape[1]
        kern = functools.partial(_ple_ln_kernel, alpha=alpha)
        in_specs = [x_spec, w_spec, pl.BlockSpec((1, tn), lambda i, j: (0, j)),
                    pl.BlockSpec((tm, kp), lambda i, j: (i, 0)),
                    pl.BlockSpec((kp, tn), lambda i, j: (0, j)),
                    col_spec, vec_full, vec_full]
        args = (x, w, bias, p, w_p, res, ln_g, ln_b)
        name = "ple_ln"
    n_out = 1 if ple is None else 2
    return pl.pallas_call(
        kern,
        grid=(m // tm, n_j),
        in_specs=in_specs,
        out_specs=[row_spec, row_spec][:n_out],
        out_shape=[jax.ShapeDtypeStruct((m, n), F32), jax.ShapeDtypeStruct((m, n), BF16)][:n_out],
        scratch_shapes=[pltpu.VMEM((n_j, tm, tn), F32)],
        compiler_params=_params(("parallel", "arbitrary")),
        name=name,
    )(*args)


def _first_argmax(x, idx, axis, big):
    m = jnp.max(x, axis=axis, keepdims=True)
    first = jnp.min(jnp.where(x == m, idx, big), axis=axis, keepdims=True)
    return m, first


def _router_kernel(x_ref, wh_ref, wl_ref, bias_ref, eidx_ref, wts_ref, rank_ref, cnt_ref, carry):
    tm = x_ref.shape[0]
    n_e = wh_ref.shape[0]
    per = n_e // N_GROUPS
    i = pl.program_id(0)

    @pl.when(i == 0)
    def _():
        carry[...] = jnp.zeros_like(carry)

    x = x_ref[...]
    xh = x.astype(BF16)
    xl = (x - xh.astype(F32)).astype(BF16)
    nt = (((1,), (1,)), ((), ()))
    logits = (lax.dot_general(wh_ref[...], xh, nt, preferred_element_type=F32)
              + lax.dot_general(wh_ref[...], xl, nt, preferred_element_type=F32)
              + lax.dot_general(wl_ref[...], xh, nt, preferred_element_type=F32))
    scores = _sigmoid(logits)
    sel = scores + bias_ref[...]

    sel3 = sel.reshape(N_GROUPS, per, tm)
    j_idx = lax.broadcasted_iota(jnp.int32, (N_GROUPS, per, tm), 1).astype(F32)
    m1, a1 = _first_argmax(sel3, j_idx, 1, float(per))
    m2 = jnp.max(jnp.where(j_idx == a1, -jnp.inf, sel3), axis=1, keepdims=True)
    gscore = (m1 + m2).reshape(N_GROUPS, tm)
    g_idx = lax.broadcasted_iota(jnp.int32, (N_GROUPS, tm), 0).astype(F32)
    gmask = jnp.zeros((N_GROUPS, tm), F32)
    work = gscore
    for _ in range(TOPK_GROUPS):
        _, ga = _first_argmax(work, g_idx, 0, float(N_GROUPS))
        hit = g_idx == ga
        gmask = jnp.where(hit, 1.0, gmask)
        work = jnp.where(hit, -jnp.inf, work)
    e_idx = lax.broadcasted_iota(jnp.int32, (n_e, tm), 0).astype(F32)
    emask = jnp.broadcast_to(gmask.reshape(N_GROUPS, 1, tm), (N_GROUPS, per, tm)).reshape(n_e, tm)
    work = jnp.where(emask > 0.5, sel, -jnp.inf)

    e_rows, s_rows, hots = [], [], []
    for _ in range(TOP_K):
        _, ea = _first_argmax(work, e_idx, 0, float(n_e))
        hit = e_idx == ea
        work = jnp.where(hit, -jnp.inf, work)
        e_rows.append(ea)
        s_rows.append(jnp.sum(jnp.where(hit, scores, 0.0), axis=0, keepdims=True))
        hots.append(jnp.where(hit, 1.0, 0.0))
    e_sel = jnp.concatenate(e_rows, axis=0)
    s_sel = jnp.concatenate(s_rows, axis=0)
    eidx_ref[...] = e_sel.astype(jnp.int32)
    wts_ref[...] = ROUTED_SCALE * s_sel / jnp.sum(s_sel, axis=0, keepdims=True)

    hot = jnp.concatenate(hots, axis=0)
    r_i = lax.broadcasted_iota(jnp.int32, (tm, tm), 0)
    c_i = lax.broadcasted_iota(jnp.int32, (tm, tm), 1)
    upper = jnp.where(r_i < c_i, 1.0, 0.0).astype(BF16)
    prefix = jnp.dot(hot.astype(BF16), upper, preferred_element_type=F32)
    base = carry[...]
    ranks = []
    for k in range(TOP_K):
        hk = hots[k]
        pk = prefix[k * n_e:(k + 1) * n_e, :] + base
        ranks.append(jnp.sum(hk * pk, axis=0, keepdims=True))
        base = base + jnp.sum(hk, axis=1, keepdims=True)
    rank_ref[...] = jnp.concatenate(ranks, axis=0).astype(jnp.int32)
    carry[...] = base
    cnt_ref[...] = base.astype(jnp.int32)


def router(h, w_hi_t, w_lo_t, bias_col):
    t, d = h.shape
    n_e = w_hi_t.shape[0]
    tm = _tile(t, 256, V7X_LANES)
    full = lambda a: pl.BlockSpec(a.shape, lambda i: (0,) * a.ndim)
    tok_spec = pl.BlockSpec((TOP_K, tm), lambda i: (0, i))
    return pl.pallas_call(
        _router_kernel,
        grid=(t // tm,),
        in_specs=[pl.BlockSpec((tm, d), lambda i: (i, 0)), full(w_hi_t), full(w_lo_t), full(bias_col)],
        out_specs=[tok_spec, tok_spec, tok_spec, pl.BlockSpec((n_e, 1), lambda i: (0, 0))],
        out_shape=[jax.ShapeDtypeStruct((TOP_K, t), jnp.int32),
                   jax.ShapeDtypeStruct((TOP_K, t), F32),
                   jax.ShapeDtypeStruct((TOP_K, t), jnp.int32),
                   jax.ShapeDtypeStruct((n_e, 1), jnp.int32)],
        scratch_shapes=[pltpu.VMEM((n_e, 1), F32)],
        compiler_params=_params(("arbitrary",)),
        name="router",
    )(h, w_hi_t, w_lo_t, bias_col)


def _moe_kernel(te_ref, nu_ref, nx_ref, tok_cur, tok_next, dst_prev, x_hbm, wg_hbm, wu_hbm, wd_ref, y_hbm,
                xbuf, ybuf, wg_st, wu_st, wgu, wdn, gsem, ssem, wsem, *, layer):
    i = pl.program_id(0)
    n_used = nu_ref[0]
    tm = xbuf.shape[1]
    f = wd_ref.shape[0]

    def weight_copies(e):
        return (pltpu.make_async_copy(wg_hbm.at[layer, e], wg_st, wsem),
                pltpu.make_async_copy(wu_hbm.at[layer, e], wu_st, wsem))
    slot = i % 2
    other = 1 - slot

    def gather_row(idx_ref, r, s):
        pltpu.make_async_copy(x_hbm.at[pl.ds(idx_ref[0, 0, r], 1)], xbuf.at[s, pl.ds(r, 1)], gsem.at[s]).start()

    def scatter_row(r, s):
        pltpu.make_async_copy(ybuf.at[s, pl.ds(r, 1)], y_hbm.at[pl.ds(dst_prev[0, 0, r], 1)], ssem.at[s]).start()

    def wait_gather(s):
        pltpu.make_async_copy(x_hbm.at[pl.ds(0, tm)], xbuf.at[s], gsem.at[s]).wait()

    def wait_scatter(s):
        pltpu.make_async_copy(ybuf.at[s], y_hbm.at[pl.ds(0, tm)], ssem.at[s]).wait()

    def loop_rows(fn):
        def body(r, carry):
            fn(r)
            return carry
        lax.fori_loop(0, tm, body, 0)

    @pl.when(i == 0)
    def _():
        ybuf[1] = jnp.zeros(ybuf.shape[1:], F32)
        loop_rows(lambda r: gather_row(tok_cur, r, 0))

    for parity in range(2):
        @pl.when(jnp.logical_and(i < n_used, slot == parity))
        def _():
            for r in range(tm):
                gather_row(tok_next, r, 1 - parity)
                scatter_row(r, 1 - parity)

    @pl.when(jnp.logical_and(i >= 1, i <= n_used))
    def _():
        wait_scatter(slot)

    @pl.when(i == 0)
    def _():
        for c in weight_copies(te_ref[0]):
            c.start()

    @pl.when(jnp.logical_and(i < n_used, jnp.logical_or(i == 0, te_ref[i] != te_ref[jnp.maximum(i - 1, 0)])))
    def _():
        for c in weight_copies(te_ref[i]):
            c.wait()

        def fuse(k0, carry):
            rows = pl.ds(pl.multiple_of(k0 * 128, 128), 128)
            wgu[rows, 0:f] = wg_st[rows, :].astype(BF16)
            wgu[rows, f:2 * f] = wu_st[rows, :].astype(BF16)
            return carry
        lax.fori_loop(0, wgu.shape[0] // 128, fuse, 0)

        def cast(k0, carry):
            rows = pl.ds(pl.multiple_of(k0 * 16, 16), 16)
            wdn[rows, :] = wd_ref[rows, :].astype(BF16)
            return carry
        lax.fori_loop(0, f // 16, cast, 0)

        @pl.when(nx_ref[i] >= 0)
        def _():
            for c in weight_copies(nx_ref[i]):
                c.start()

    @pl.when(i < n_used)
    def _():
        wait_gather(slot)
        rows = tm // MOE_ROW_GROUPS
        for h in range(MOE_ROW_GROUPS):
            rs = slice(h * rows, (h + 1) * rows)
            gu = jnp.dot(xbuf[slot, rs, :].astype(BF16), wgu[...], preferred_element_type=F32)
            g = gu[:, 0:f]
            a = (g * _sigmoid(g) * gu[:, f:2 * f]).astype(BF16)
            ybuf[slot, rs, :] = jnp.dot(a, wdn[...], preferred_element_type=F32)

    @pl.when(i == n_used)
    def _():
        wait_gather(slot)
        loop_rows(lambda r: scatter_row(r, other))
        wait_scatter(other)


def moe_grouped(x, dst_all, tile_expert, n_used, next_expert, wg, wu, wd, layer, tm, out_rows):
    n_grid = tile_expert.shape[0]
    t, d = x.shape
    f = wd.shape[2]
    dst3 = dst_all.reshape(n_grid + 1, 1, tm)
    tok3 = dst3 % t
    smem_blk = lambda fn: pl.BlockSpec((1, 1, tm), fn, memory_space=pltpu.SMEM)
    grid_spec = pltpu.PrefetchScalarGridSpec(
        num_scalar_prefetch=3,
        grid=(n_grid,),
        in_specs=[smem_blk(lambda i, te, nu, nx: (i + 1, 0, 0)),
                  smem_blk(lambda i, te, nu, nx: (jnp.minimum(i + 2, n_grid), 0, 0)),
                  smem_blk(lambda i, te, nu, nx: (i, 0, 0)),
                  pl.BlockSpec(memory_space=pl.ANY),
                  pl.BlockSpec(memory_space=pl.ANY),
                  pl.BlockSpec(memory_space=pl.ANY),
                  pl.BlockSpec((None, None, f, d), lambda i, te, nu, nx: (layer, te[i], 0, 0))],
        out_specs=pl.BlockSpec(memory_space=pl.ANY),
        scratch_shapes=[pltpu.VMEM((2, tm, d), F32), pltpu.VMEM((2, tm, d), F32),
                        pltpu.VMEM((d, f), F32), pltpu.VMEM((d, f), F32),
                        pltpu.VMEM((d, 2 * f), BF16), pltpu.VMEM((f, d), BF16),
                        pltpu.SemaphoreType.DMA((2,)), pltpu.SemaphoreType.DMA((2,)),
                        pltpu.SemaphoreType.DMA(())],
    )
    return pl.pallas_call(
        functools.partial(_moe_kernel, layer=layer),
        grid_spec=grid_spec,
        out_shape=jax.ShapeDtypeStruct((out_rows, d), F32),
        compiler_params=_params(("arbitrary",)),
        name="moe_grouped",
    )(tile_expert, n_used, next_expert, tok3, tok3, dst3, x, wg, wu, wd)


def _shared_kernel(x_ref, wgu_ref, wd_ref, o_ref):
    f = wd_ref.shape[0]
    rows = x_ref.shape[0] // MOE_ROW_GROUPS
    for h in range(MOE_ROW_GROUPS):
        rs = slice(h * rows, (h + 1) * rows)
        gu = jnp.dot(x_ref[rs, :].astype(BF16), wgu_ref[...], preferred_element_type=F32)
        g = gu[:, 0:f]
        a = (g * _sigmoid(g) * gu[:, f:2 * f]).astype(BF16)
        o_ref[rs, :] = jnp.dot(a, wd_ref[...], preferred_element_type=F32)


def shared_swiglu(x, wgu, wd, layer):
    t, d = x.shape
    f = wd.shape[1]
    tm = _tile(t, 256, V7X_SUBLANES * 2)
    return pl.pallas_call(
        _shared_kernel,
        grid=(t // tm,),
        in_specs=[pl.BlockSpec((tm, d), lambda i: (i, 0)),
                  pl.BlockSpec((None, d, 2 * f), lambda i: (layer, 0, 0)),
                  pl.BlockSpec((None, f, d), lambda i: (layer, 0, 0))],
        out_specs=pl.BlockSpec((tm, d), lambda i: (i, 0)),
        out_shape=jax.ShapeDtypeStruct((t, d), F32),
        compiler_params=_params(("parallel",)),
        name="shared_swiglu",
    )(x, wgu, wd)


def _combine_kernel(wts_ref, h_ref, sh_ref, *rest, alpha):
    planes = rest[:TOP_K]
    g_ref, b_ref, o32_ref, o16_ref = rest[TOP_K:]
    acc = alpha * h_ref[...] + sh_ref[...]
    for k in range(TOP_K):
        acc = acc + wts_ref[:, k:k + 1] * planes[k][...]
    mu = jnp.mean(acc, axis=-1, keepdims=True)
    c = acc - mu
    var = jnp.mean(c * c, axis=-1, keepdims=True)
    y = c * lax.rsqrt(var + LN_EPS) * g_ref[...] + b_ref[...]
    o32_ref[...] = y
    o16_ref[...] = y.astype(BF16)


def combine_ln(h, y_shared, y_slots, wts, ln_g, ln_b, alpha):
    t, d = h.shape
    tm = _tile(t, 128, V7X_SUBLANES * 2)
    n_i = t // tm
    row = pl.BlockSpec((tm, d), lambda i: (i, 0))
    vec = pl.BlockSpec((1, d), lambda i: (0, 0))
    plane = lambda k: pl.BlockSpec((tm, d), lambda i: (k * n_i + i, 0))
    return pl.pallas_call(
        functools.partial(_combine_kernel, alpha=alpha),
        grid=(n_i,),
        in_specs=[pl.BlockSpec((tm, TOP_K), lambda i: (i, 0)), row, plane(0)]
        + [plane(k) for k in range(TOP_K)] + [vec, vec],
        out_specs=[row, row],
        out_shape=[jax.ShapeDtypeStruct((t, d), F32), jax.ShapeDtypeStruct((t, d), BF16)],
        compiler_params=_params(("parallel",)),
        name="combine_ln",
    )(wts, h, y_shared, *([y_slots] * TOP_K), ln_g, ln_b)


def _dispatch_plan(eidx_t, rank_t, counts, tm, n_grid):
    t = eidx_t.shape[1]
    n_e = counts.shape[0]
    padded = ((counts + tm - 1) // tm) * tm
    ends = jnp.cumsum(padded)
    starts = ends - padded
    hit = eidx_t[:, :, None] == jnp.arange(n_e, dtype=jnp.int32)
    pos = jnp.sum(jnp.where(hit, starts, 0), axis=-1) + rank_t
    slot = (jnp.arange(TOP_K, dtype=jnp.int32)[:, None] * t + jnp.arange(t, dtype=jnp.int32)[None, :])
    spare = TOP_K * t + jnp.arange((n_grid + 1) * tm, dtype=jnp.int32) % tm
    dst_all = spare.at[(tm + pos).reshape(-1)].set(slot.reshape(-1), unique_indices=True,
                                                   mode="promise_in_bounds")
    tile_start = jnp.arange(n_grid, dtype=jnp.int32) * tm
    tile_expert = jnp.minimum(jnp.sum(ends[None, :] <= tile_start[:, None], axis=1), n_e - 1).astype(jnp.int32)
    n_used = (ends[-1] // tm).astype(jnp.int32).reshape(1)
    ids = jnp.arange(n_e, dtype=jnp.int32)
    later_used = jnp.logical_and(ids[None, :] > ids[:, None], padded[None, :] > 0)
    nxt = jnp.min(jnp.where(later_used, ids[None, :], n_e), axis=1)
    nxt = jnp.where(nxt == n_e, -1, nxt)
    next_expert = jnp.sum(jnp.where(tile_expert[:, None] == ids[None, :], nxt[None, :], 0), axis=1)
    return dst_all, tile_expert, n_used, next_expert.astype(jnp.int32)


def kernel(x, p, w_in, w_pool, pool_scale, gmlp_ln_g, gmlp_ln_b, w_spatial, b_spatial, w_conv, norm_g_gmlp,
           norm_g_conv, w_out, ln1_g, ln1_b, w_router, router_bias, w_exp_gate, w_exp_up, w_exp_down,
           w_sh_gate, w_sh_up, w_sh_down, ln2_g, ln2_b, w_ple_gate, b_ple_gate, w_ple_proj, ln3_g, ln3_b):
    depth = w_in.shape[0]
    batch, seq, d = x.shape
    t = batch * seq
    n_e = w_router.shape[2]
    alpha = (2.0 * depth) ** 0.25
    row = lambda a: a.reshape(1, -1)

    tm_e = _tile(t * TOP_K // n_e, 256, V7X_SUBLANES * 2)
    n_grid = t * TOP_K // tm_e + n_e

    sgu16 = jnp.concatenate([w_sh_gate, w_sh_up], axis=-1).astype(BF16)
    sd16 = w_sh_down.astype(BF16)

    h32 = x.reshape(t, d)
    h16 = h32.astype(BF16)
    p2 = p.reshape(depth, t, p.shape[-1])
    for i in range(depth):
        z = matmul(h16, w_in[i].astype(BF16), BF16)
        ycat = mixer(z, batch, seq, w_pool[i].astype(BF16), row(pool_scale[i]), row(gmlp_ln_g[i]),
                     row(gmlp_ln_b[i]), w_spatial[i], b_spatial[i].T, w_conv[i], row(norm_g_gmlp[i]),
                     row(norm_g_conv[i]))
        h32, = proj_ln(ycat, w_out[i].astype(BF16), h32, row(ln1_g[i]), row(ln1_b[i]), alpha)
        wr_t = w_router[i].T
        wr_hi = wr_t.astype(BF16)
        wr_lo = (wr_t - wr_hi.astype(F32)).astype(BF16)
        eidx_t, wts_t, rank_t, counts = router(h32, wr_hi, wr_lo, router_bias[i].reshape(n_e, 1))
        dst_all, tile_expert, n_used, next_expert = _dispatch_plan(eidx_t, rank_t, counts[:, 0], tm_e, n_grid)
        y_slots = moe_grouped(h32, dst_all, tile_expert, n_used, next_expert, w_exp_gate, w_exp_up, w_exp_down,
                              i, tm_e, TOP_K * t + tm_e)
        y_shared = shared_swiglu(h32, sgu16, sd16, i)
        h32, h16 = combine_ln(h32, y_shared, y_slots, wts_t.T, row(ln2_g[i]), row(ln2_b[i]), alpha)
        h32, h16 = proj_ln(h16, w_ple_gate[i].astype(BF16), h32, row(ln3_g[i]), row(ln3_b[i]), alpha,
                           ple=(row(b_ple_gate[i]), p2[i], w_ple_proj[i].astype(BF16)))
    return h32.reshape(batch, seq, d)
```

```python
import functools
import math

import jax
import jax.numpy as jnp
from jax import lax
from jax.experimental import pallas as pl
from jax.experimental.pallas import tpu as pltpu

F32 = jnp.float32
BF16 = jnp.bfloat16

POOL_WINDOWS = (2, 4, 8, 16)
TOP_K = 8
N_GROUPS = 8
TOPK_GROUPS = 4
ROUTED_SCALE = 2.5
LN_EPS = 1e-5
RMS_EPS = 1e-6

V7X_LANES = 128
V7X_SUBLANES = 8
V7X_VMEM_LIMIT_BYTES = 60 * 1024 * 1024
MAX_HALO = 16
MOE_ROW_GROUPS = 2
def _tile(n, pref, align):
    t = min(pref, n)
    t -= t % align
    while t >= align:
        if n % t == 0:
            return t
        t -= align
    return n


def _params(sem, flags=None):
    return pltpu.CompilerParams(dimension_semantics=sem, vmem_limit_bytes=V7X_VMEM_LIMIT_BYTES, flags=flags)


def _sigmoid(x):
    return 1.0 / (1.0 + jnp.exp(-x))


def _matmul_kernel(x_ref, w_ref, o_ref):
    o_ref[...] = jnp.dot(x_ref[...], w_ref[...], preferred_element_type=F32).astype(o_ref.dtype)


def matmul(x, w, out_dtype):
    m, k = x.shape
    n = w.shape[1]
    tm = _tile(m, 1024, V7X_SUBLANES * 2)
    tn = _tile(n, 512, V7X_LANES)
    return pl.pallas_call(
        _matmul_kernel,
        grid=(m // tm, n // tn),
        in_specs=[pl.BlockSpec((tm, k), lambda i, j: (i, 0)),
                  pl.BlockSpec((k, tn), lambda i, j: (0, j))],
        out_specs=pl.BlockSpec((tm, tn), lambda i, j: (i, j)),
        out_shape=jax.ShapeDtypeStruct((m, n), out_dtype),
        compiler_params=_params(("parallel", "arbitrary")),
        name="in_proj",
    )(x, w)


def _mixer_kernel(z_ref, wpool_ref, pscale_ref, lng_ref, lnb_ref, ws_ref, bs_ref, wconv_ref,
                  ngg_ref, ngc_ref, o_ref, pool_carry, conv_carry, gm_buf, *, dims):
    pool_w, gdim, gm_w, chunk, conv_w = dims
    ts = z_ref.shape[0]
    s = pl.program_id(1)
    n_groups = pool_w // gdim
    heads = gm_w // chunk

    @pl.when(s == 0)
    def _():
        pool_carry[...] = jnp.zeros_like(pool_carry)
        conv_carry[...] = jnp.zeros_like(conv_carry)

    pos = (s * ts + lax.broadcasted_iota(jnp.int32, (ts, 1), 0)).astype(F32)

    a = z_ref[:, 0:pool_w].astype(F32)
    ext = jnp.concatenate([pool_carry[...], a], axis=0)
    pool_carry[...] = a[ts - MAX_HALO:, :]
    ys = []
    ssq = jnp.zeros((ts, 1), F32)
    for g in range(n_groups):
        win = POOL_WINDOWS[g]
        acc = ext[:, g * gdim:(g + 1) * gdim]
        span = 1
        while span < win:
            acc = acc + pltpu.roll(acc, span, axis=0)
            span *= 2
        count = jnp.minimum(pos + 1.0, float(win))
        d = acc[MAX_HALO:, :] / count - a[:, g * gdim:(g + 1) * gdim]
        y = jnp.dot(d.astype(BF16), wpool_ref[g], preferred_element_type=F32)
        ssq = ssq + jnp.sum(y * y, axis=-1, keepdims=True)
        ys.append(y)
    rinv = lax.rsqrt(ssq / pool_w + RMS_EPS)
    for g in range(n_groups):
        o_ref[:, g * gdim:(g + 1) * gdim] = (
            ys[g] * rinv * pscale_ref[:, g * gdim:(g + 1) * gdim]).astype(o_ref.dtype)

    uv = jax.nn.gelu(z_ref[:, pool_w:pool_w + 2 * gm_w].astype(F32))
    u = uv[:, :gm_w]
    v = uv[:, gm_w:]
    mu = jnp.mean(v, axis=-1, keepdims=True)
    vc = v - mu
    var = jnp.mean(vc * vc, axis=-1, keepdims=True)
    vn = (vc * lax.rsqrt(var + LN_EPS) * lng_ref[...] + lnb_ref[...]).astype(BF16)
    row = lax.broadcasted_iota(jnp.int32, (chunk, chunk), 0)
    col = lax.broadcasted_iota(jnp.int32, (chunk, chunk), 1)
    ssq = jnp.zeros((ts, 1), F32)
    for h in range(heads):
        wsm = jnp.where(row >= col, ws_ref[h], 0.0).astype(BF16)
        bias = bs_ref[:, h:h + 1]
        parts = []
        for c in range(ts // chunk):
            blk = vn[c * chunk:(c + 1) * chunk, h * chunk:(h + 1) * chunk]
            parts.append(jnp.dot(wsm, blk, preferred_element_type=F32) + bias)
        sv = jnp.concatenate(parts, axis=0) if len(parts) > 1 else parts[0]
        y = u[:, h * chunk:(h + 1) * chunk] * sv
        ssq = ssq + jnp.sum(y * y, axis=-1, keepdims=True)
        gm_buf[:, h * chunk:(h + 1) * chunk] = y
    rinv = lax.rsqrt(ssq / gm_w + RMS_EPS)
    o_ref[:, pool_w:pool_w + gm_w] = (gm_buf[...] * rinv * ngg_ref[...]).astype(o_ref.dtype)

    off = pool_w + 2 * gm_w
    bg = z_ref[:, off:off + conv_w].astype(F32)
    x = z_ref[:, off + conv_w:off + 2 * conv_w].astype(F32) * z_ref[:, off + 2 * conv_w:off + 3 * conv_w].astype(F32)
    ext = jnp.concatenate([conv_carry[...], x], axis=0)
    conv_carry[...] = x[ts - V7X_SUBLANES:, :]
    zc = (wconv_ref[0:1, :] * pltpu.roll(ext, 2, axis=0)[V7X_SUBLANES:, :]
          + wconv_ref[1:2, :] * pltpu.roll(ext, 1, axis=0)[V7X_SUBLANES:, :]
          + wconv_ref[2:3, :] * x)
    y = bg * zc
    rinv = lax.rsqrt(jnp.mean(y * y, axis=-1, keepdims=True) + RMS_EPS)
    o_ref[:, pool_w + gm_w:pool_w + gm_w + conv_w] = (y * rinv * ngc_ref[...]).astype(o_ref.dtype)


def mixer(z, batch, seq, w_pool, pool_scale, ln_g, ln_b, w_spatial, b_spatial_t, w_conv, ng_gmlp, ng_conv):
    n_groups, gdim, _ = w_pool.shape
    pool_w = n_groups * gdim
    heads, chunk, _ = w_spatial.shape
    gm_w = heads * chunk
    conv_w = w_conv.shape[1]
    assert w_conv.shape[0] == 3 and n_groups == len(POOL_WINDOWS)
    width = z.shape[1]
    assert width == pool_w + 2 * gm_w + 3 * conv_w
    mix_w = pool_w + gm_w + conv_w
    ts = _tile(seq, 256, chunk)
    n_s = seq // ts
    full = lambda a: pl.BlockSpec(a.shape, lambda b, s: (0,) * a.ndim)
    kern = functools.partial(_mixer_kernel, dims=(pool_w, gdim, gm_w, chunk, conv_w))
    return pl.pallas_call(
        kern,
        grid=(batch, n_s),
        in_specs=[pl.BlockSpec((ts, width), lambda b, s: (b * n_s + s, 0)),
                  full(w_pool), full(pool_scale), full(ln_g), full(ln_b), full(w_spatial),
                  full(b_spatial_t), full(w_conv), full(ng_gmlp), full(ng_conv)],
        out_specs=pl.BlockSpec((ts, mix_w), lambda b, s: (b * n_s + s, 0)),
        out_shape=jax.ShapeDtypeStruct((batch * seq, mix_w), BF16),
        scratch_shapes=[pltpu.VMEM((MAX_HALO, pool_w), F32),
                        pltpu.VMEM((V7X_SUBLANES, conv_w), F32),
                        pltpu.VMEM((ts, gm_w), F32)],
        compiler_params=_params(("arbitrary", "arbitrary")),
        name="mixer",
    )(z, w_pool, pool_scale, ln_g, ln_b, w_spatial, b_spatial_t, w_conv, ng_gmlp, ng_conv)


def _layer_norm_store(acc, n_j, tn, n, g_ref, b_ref, o32_ref, o16_ref=None):
    tot = jnp.sum(acc[0], axis=-1, keepdims=True)
    for jj in range(1, n_j):
        tot = tot + jnp.sum(acc[jj], axis=-1, keepdims=True)
    mu = tot / n
    c0 = acc[0] - mu
    sq = jnp.sum(c0 * c0, axis=-1, keepdims=True)
    for jj in range(1, n_j):
        c = acc[jj] - mu
        sq = sq + jnp.sum(c * c, axis=-1, keepdims=True)
    rstd = lax.rsqrt(sq / n + LN_EPS)

    for jj in range(n_j):
        sl = slice(jj * tn, (jj + 1) * tn)
        y = (acc[jj] - mu) * rstd * g_ref[:, sl] + b_ref[:, sl]
        o32_ref[:, sl] = y
        if o16_ref is not None:
            o16_ref[:, sl] = y.astype(BF16)


def _proj_ln_kernel(x_ref, w_ref, res_ref, g_ref, b_ref, o32_ref, acc, *, alpha):
    j = pl.program_id(1)
    n_j, _, tn = acc.shape
    y = jnp.dot(x_ref[...], w_ref[...], preferred_element_type=F32)
    acc[j] = alpha * res_ref[...] + y

    @pl.when(j == n_j - 1)
    def _():
        _layer_norm_store(acc, n_j, tn, n_j * tn, g_ref, b_ref, o32_ref)


def _ple_ln_kernel(x_ref, w_ref, bias_ref, p_ref, wp_ref, res_ref, g_ref, b_ref, o32_ref, o16_ref, acc,
                   *, alpha):
    j = pl.program_id(1)
    n_j, _, tn = acc.shape
    gate = _sigmoid(jnp.dot(x_ref[...], w_ref[...], preferred_element_type=F32) + bias_ref[...])
    proj = jnp.dot(p_ref[...].astype(BF16), wp_ref[...], preferred_element_type=F32)
    acc[j] = alpha * res_ref[...] + gate * proj

    @pl.when(j == n_j - 1)
    def _():
        _layer_norm_store(acc, n_j, tn, n_j * tn, g_ref, b_ref, o32_ref, o16_ref)


def proj_ln(x, w, res, ln_g, ln_b, alpha, ple=None):
    m, k = x.shape
    n = w.shape[1]
    tm = _tile(m, 512, V7X_SUBLANES * 2)
    tn = _tile(n, 512, V7X_LANES)
    n_j = n // tn
    x_spec = pl.BlockSpec((tm, k), lambda i, j: (i, 0))
    w_spec = pl.BlockSpec((k, tn), lambda i, j: (0, j))
    col_spec = pl.BlockSpec((tm, tn), lambda i, j: (i, j))
    vec_full = pl.BlockSpec((1, n), lambda i, j: (0, 0))
    row_spec = pl.BlockSpec((tm, n), lambda i, j: (i, 0), pipeline_mode=pl.Buffered(1))
    if ple is None:
        kern = functools.partial(_proj_ln_kernel, alpha=alpha)
        in_specs = [x_spec, w_spec, col_spec, vec_full, vec_full]
        args = (x, w, res, ln_g, ln_b)
        name = "out_proj_ln"
    else:
        bias, p, w_p = ple
        kp = p.shape[1]
        kern = functools.partial(_ple_ln_kernel, alpha=alpha)
        in_specs = [x_spec, w_spec, pl.BlockSpec((1, tn), lambda i, j: (0, j)),
                    pl.BlockSpec((tm, kp), lambda i, j: (i, 0)),
                    pl.BlockSpec((kp, tn), lambda i, j: (0, j)),
                    col_spec, vec_full, vec_full]
        args = (x, w, bias, p, w_p, res, ln_g, ln_b)
        name = "ple_ln"
    n_out = 1 if ple is None else 2
    return pl.pallas_call(
        kern,
        grid=(m // tm, n_j),
        in_specs=in_specs,
        out_specs=[row_spec, row_spec][:n_out],
        out_shape=[jax.ShapeDtypeStruct((m, n), F32), jax.ShapeDtypeStruct((m, n), BF16)][:n_out],
        scratch_shapes=[pltpu.VMEM((n_j, tm, tn), F32)],
        compiler_params=_params(("parallel", "arbitrary")),
        name=name,
    )(*args)


def _first_argmax(x, idx, axis, big):
    m = jnp.max(x, axis=axis, keepdims=True)
    first = jnp.min(jnp.where(x == m, idx, big), axis=axis, keepdims=True)
    return m, first


def _router_kernel(x_ref, wh_ref, wl_ref, bias_ref, eidx_ref, wts_ref, rank_ref, cnt_ref, carry):
    tm = x_ref.shape[0]
    n_e = wh_ref.shape[0]
    per = n_e // N_GROUPS
    i = pl.program_id(0)

    @pl.when(i == 0)
    def _():
        carry[...] = jnp.zeros_like(carry)

    x = x_ref[...]
    xh = x.astype(BF16)
    xl = (x - xh.astype(F32)).astype(BF16)
    nt = (((1,), (1,)), ((), ()))
    logits = (lax.dot_general(wh_ref[...], xh, nt, preferred_element_type=F32)
              + lax.dot_general(wh_ref[...], xl, nt, preferred_element_type=F32)
              + lax.dot_general(wl_ref[...], xh, nt, preferred_element_type=F32))
    scores = _sigmoid(logits)
    sel = scores + bias_ref[...]

    sel3 = sel.reshape(N_GROUPS, per, tm)
    j_idx = lax.broadcasted_iota(jnp.int32, (N_GROUPS, per, tm), 1).astype(F32)
    m1, a1 = _first_argmax(sel3, j_idx, 1, float(per))
    m2 = jnp.max(jnp.where(j_idx == a1, -jnp.inf, sel3), axis=1, keepdims=True)
    gscore = (m1 + m2).reshape(N_GROUPS, tm)
    g_idx = lax.broadcasted_iota(jnp.int32, (N_GROUPS, tm), 0).astype(F32)
    gmask = jnp.zeros((N_GROUPS, tm), F32)
    work = gscore
    for _ in range(TOPK_GROUPS):
        _, ga = _first_argmax(work, g_idx, 0, float(N_GROUPS))
        hit = g_idx == ga
        gmask = jnp.where(hit, 1.0, gmask)
        work = jnp.where(hit, -jnp.inf, work)
    e_idx = lax.broadcasted_iota(jnp.int32, (n_e, tm), 0).astype(F32)
    emask = jnp.broadcast_to(gmask.reshape(N_GROUPS, 1, tm), (N_GROUPS, per, tm)).reshape(n_e, tm)
    work = jnp.where(emask > 0.5, sel, -jnp.inf)

    e_rows, s_rows, hots = [], [], []
    for _ in range(TOP_K):
        _, ea = _first_argmax(work, e_idx, 0, float(n_e))
        hit = e_idx == ea
        work = jnp.where(hit, -jnp.inf, work)
        e_rows.append(ea)
        s_rows.append(jnp.sum(jnp.where(hit, scores, 0.0), axis=0, keepdims=True))
        hots.append(jnp.where(hit, 1.0, 0.0))
    e_sel = jnp.concatenate(e_rows, axis=0)
    s_sel = jnp.concatenate(s_rows, axis=0)
    eidx_ref[...] = e_sel.astype(jnp.int32)
    wts_ref[...] = ROUTED_SCALE * s_sel / jnp.sum(s_sel, axis=0, keepdims=True)

    hot = jnp.concatenate(hots, axis=0)
    r_i = lax.broadcasted_iota(jnp.int32, (tm, tm), 0)
    c_i = lax.broadcasted_iota(jnp.int32, (tm, tm), 1)
    upper = jnp.where(r_i < c_i, 1.0, 0.0).astype(BF16)
    prefix = jnp.dot(hot.astype(BF16), upper, preferred_element_type=F32)
    base = carry[...]
    ranks = []
    for k in range(TOP_K):
        hk = hots[k]
        pk = prefix[k * n_e:(k + 1) * n_e, :] + base
        ranks.append(jnp.sum(hk * pk, axis=0, keepdims=True))
        base = base + jnp.sum(hk, axis=1, keepdims=True)
    rank_ref[...] = jnp.concatenate(ranks, axis=0).astype(jnp.int32)
    carry[...] = base
    cnt_ref[...] = base.astype(jnp.int32)


def router(h, w_hi_t, w_lo_t, bias_col):
    t, d = h.shape
    n_e = w_hi_t.shape[0]
    tm = _tile(t, 256, V7X_LANES)
    full = lambda a: pl.BlockSpec(a.shape, lambda i: (0,) * a.ndim)
    tok_spec = pl.BlockSpec((TOP_K, tm), lambda i: (0, i))
    return pl.pallas_call(
        _router_kernel,
        grid=(t // tm,),
        in_specs=[pl.BlockSpec((tm, d), lambda i: (i, 0)), full(w_hi_t), full(w_lo_t), full(bias_col)],
        out_specs=[tok_spec, tok_spec, tok_spec, pl.BlockSpec((n_e, 1), lambda i: (0, 0))],
        out_shape=[jax.ShapeDtypeStruct((TOP_K, t), jnp.int32),
                   jax.ShapeDtypeStruct((TOP_K, t), F32),
                   jax.ShapeDtypeStruct((TOP_K, t), jnp.int32),
                   jax.ShapeDtypeStruct((n_e, 1), jnp.int32)],
        scratch_shapes=[pltpu.VMEM((n_e, 1), F32)],
        compiler_params=_params(("arbitrary",)),
        name="router",
    )(h, w_hi_t, w_lo_t, bias_col)


def _moe_kernel(te_ref, nu_ref, nx_ref, tok_cur, tok_next, dst_prev, x_hbm, wg_hbm, wu_hbm, wd_ref, y_hbm,
                xbuf, ybuf, wg_st, wu_st, wgu, wdn, gsem, ssem, wsem, *, layer):
    i = pl.program_id(0)
    n_used = nu_ref[0]
    tm = xbuf.shape[1]
    f = wd_ref.shape[0]

    def weight_copies(e):
        return (pltpu.make_async_copy(wg_hbm.at[layer, e], wg_st, wsem),
                pltpu.make_async_copy(wu_hbm.at[layer, e], wu_st, wsem))
    slot = i % 2
    other = 1 - slot

    def gather_row(idx_ref, r, s, priority=0):
        pltpu.make_async_copy(x_hbm.at[pl.ds(idx_ref[0, 0, r], 1)], xbuf.at[s, pl.ds(r, 1)],
                              gsem.at[s]).start(priority=priority)

    def scatter_row(r, s, priority=0):
        pltpu.make_async_copy(ybuf.at[s, pl.ds(r, 1)], y_hbm.at[pl.ds(dst_prev[0, 0, r], 1)],
                              ssem.at[s]).start(priority=priority)

    def wait_gather(s):
        pltpu.make_async_copy(x_hbm.at[pl.ds(0, tm)], xbuf.at[s], gsem.at[s]).wait()

    def wait_scatter(s):
        pltpu.make_async_copy(ybuf.at[s], y_hbm.at[pl.ds(0, tm)], ssem.at[s]).wait()

    def loop_rows(fn):
        def body(r, carry):
            fn(r)
            return carry
        lax.fori_loop(0, tm, body, 0)

    @pl.when(i == 0)
    def _():
        ybuf[1] = jnp.zeros(ybuf.shape[1:], F32)
        loop_rows(lambda r: gather_row(tok_cur, r, 0))

    for parity in range(2):
        @pl.when(jnp.logical_and(i < n_used, slot == parity))
        def _():
            for r in range(tm):
                gather_row(tok_next, r, 1 - parity, priority=r % 2)
                scatter_row(r, 1 - parity, priority=r % 2)

    @pl.when(jnp.logical_and(i >= 1, i <= n_used))
    def _():
        wait_scatter(slot)

    @pl.when(i == 0)
    def _():
        for c in weight_copies(te_ref[0]):
            c.start()

    @pl.when(jnp.logical_and(i < n_used, jnp.logical_or(i == 0, te_ref[i] != te_ref[jnp.maximum(i - 1, 0)])))
    def _():
        for c in weight_copies(te_ref[i]):
            c.wait()

        def fuse(k0, carry):
            rows = pl.ds(pl.multiple_of(k0 * 128, 128), 128)
            wgu[rows, 0:f] = wg_st[rows, :].astype(BF16)
            wgu[rows, f:2 * f] = wu_st[rows, :].astype(BF16)
            return carry
        lax.fori_loop(0, wgu.shape[0] // 128, fuse, 0)

        def cast(k0, carry):
            rows = pl.ds(pl.multiple_of(k0 * 16, 16), 16)
            wdn[rows, :] = wd_ref[rows, :].astype(BF16)
            return carry
        lax.fori_loop(0, f // 16, cast, 0)

        @pl.when(nx_ref[i] >= 0)
        def _():
            for c in weight_copies(nx_ref[i]):
                c.start()

    @pl.when(i < n_used)
    def _():
        wait_gather(slot)
        rows = tm // MOE_ROW_GROUPS
        for h in range(MOE_ROW_GROUPS):
            rs = slice(h * rows, (h + 1) * rows)
            gu = jnp.dot(xbuf[slot, rs, :].astype(BF16), wgu[...], preferred_element_type=F32)
            g = gu[:, 0:f]
            a = (g * _sigmoid(g) * gu[:, f:2 * f]).astype(BF16)
            ybuf[slot, rs, :] = jnp.dot(a, wdn[...], preferred_element_type=F32)

    @pl.when(i == n_used)
    def _():
        wait_gather(slot)
        loop_rows(lambda r: scatter_row(r, other))
        wait_scatter(other)


def moe_grouped(x, dst_all, tile_expert, n_used, next_expert, wg, wu, wd, layer, tm, out_rows):
    n_grid = tile_expert.shape[0]
    t, d = x.shape
    f = wd.shape[2]
    dst3 = dst_all.reshape(n_grid + 1, 1, tm)
    tok3 = dst3 % t
    smem_blk = lambda fn: pl.BlockSpec((1, 1, tm), fn, memory_space=pltpu.SMEM)
    grid_spec = pltpu.PrefetchScalarGridSpec(
        num_scalar_prefetch=3,
        grid=(n_grid,),
        in_specs=[smem_blk(lambda i, te, nu, nx: (i + 1, 0, 0)),
                  smem_blk(lambda i, te, nu, nx: (jnp.minimum(i + 2, n_grid), 0, 0)),
                  smem_blk(lambda i, te, nu, nx: (i, 0, 0)),
                  pl.BlockSpec(memory_space=pl.ANY),
                  pl.BlockSpec(memory_space=pl.ANY),
                  pl.BlockSpec(memory_space=pl.ANY),
                  pl.BlockSpec((None, None, f, d), lambda i, te, nu, nx: (layer, te[i], 0, 0))],
        out_specs=pl.BlockSpec(memory_space=pl.ANY),
        scratch_shapes=[pltpu.VMEM((2, tm, d), F32), pltpu.VMEM((2, tm, d), F32),
                        pltpu.VMEM((d, f), F32), pltpu.VMEM((d, f), F32),
                        pltpu.VMEM((d, 2 * f), BF16), pltpu.VMEM((f, d), BF16),
                        pltpu.SemaphoreType.DMA((2,)), pltpu.SemaphoreType.DMA((2,)),
                        pltpu.SemaphoreType.DMA(())],
    )
    return pl.pallas_call(
        functools.partial(_moe_kernel, layer=layer),
        grid_spec=grid_spec,
        out_shape=jax.ShapeDtypeStruct((out_rows, d), F32),
        compiler_params=_params(("arbitrary",)),
        name="moe_grouped",
    )(tile_expert, n_used, next_expert, tok3, tok3, dst3, x, wg, wu, wd)


def _shared_kernel(x_ref, wgu_ref, wd_ref, o_ref):
    f = wd_ref.shape[0]
    rows = x_ref.shape[0] // MOE_ROW_GROUPS
    for h in range(MOE_ROW_GROUPS):
        rs = slice(h * rows, (h + 1) * rows)
        gu = jnp.dot(x_ref[rs, :].astype(BF16), wgu_ref[...], preferred_element_type=F32)
        g = gu[:, 0:f]
        a = (g * _sigmoid(g) * gu[:, f:2 * f]).astype(BF16)
        o_ref[rs, :] = jnp.dot(a, wd_ref[...], preferred_element_type=F32)


def shared_swiglu(x, wgu, wd, layer):
    t, d = x.shape
    f = wd.shape[1]
    tm = _tile(t, 256, V7X_SUBLANES * 2)
    return pl.pallas_call(
        _shared_kernel,
        grid=(t // tm,),
        in_specs=[pl.BlockSpec((tm, d), lambda i: (i, 0)),
                  pl.BlockSpec((None, d, 2 * f), lambda i: (layer, 0, 0)),
                  pl.BlockSpec((None, f, d), lambda i: (layer, 0, 0))],
        out_specs=pl.BlockSpec((tm, d), lambda i: (i, 0)),
        out_shape=jax.ShapeDtypeStruct((t, d), F32),
        compiler_params=_params(("parallel",)),
        name="shared_swiglu",
    )(x, wgu, wd)


def _combine_kernel(wts_ref, h_ref, sh_ref, *rest, alpha):
    planes = rest[:TOP_K]
    g_ref, b_ref, o32_ref, o16_ref = rest[TOP_K:]
    acc = alpha * h_ref[...] + sh_ref[...]
    for k in range(TOP_K):
        acc = acc + wts_ref[:, k:k + 1] * planes[k][...]
    mu = jnp.mean(acc, axis=-1, keepdims=True)
    c = acc - mu
    var = jnp.mean(c * c, axis=-1, keepdims=True)
    y = c * lax.rsqrt(var + LN_EPS) * g_ref[...] + b_ref[...]
    o32_ref[...] = y
    o16_ref[...] = y.astype(BF16)


def combine_ln(h, y_shared, y_slots, wts, ln_g, ln_b, alpha):
    t, d = h.shape
    tm = _tile(t, 128, V7X_SUBLANES * 2)
    n_i = t // tm
    row = pl.BlockSpec((tm, d), lambda i: (i, 0))
    vec = pl.BlockSpec((1, d), lambda i: (0, 0))
    plane = lambda k: pl.BlockSpec((tm, d), lambda i: (k * n_i + i, 0))
    return pl.pallas_call(
        functools.partial(_combine_kernel, alpha=alpha),
        grid=(n_i,),
        in_specs=[pl.BlockSpec((tm, TOP_K), lambda i: (i, 0)), row, plane(0)]
        + [plane(k) for k in range(TOP_K)] + [vec, vec],
        out_specs=[row, row],
        out_shape=[jax.ShapeDtypeStruct((t, d), F32), jax.ShapeDtypeStruct((t, d), BF16)],
        compiler_params=_params(("parallel",)),
        name="combine_ln",
    )(wts, h, y_shared, *([y_slots] * TOP_K), ln_g, ln_b)


def _dispatch_plan(eidx_t, rank_t, counts, tm, n_grid):
    t = eidx_t.shape[1]
    n_e = counts.shape[0]
    padded = ((counts + tm - 1) // tm) * tm
    ends = jnp.cumsum(padded)
    starts = ends - padded
    hit = eidx_t[:, :, None] == jnp.arange(n_e, dtype=jnp.int32)
    pos = jnp.sum(jnp.where(hit, starts, 0), axis=-1) + rank_t
    slot = (jnp.arange(TOP_K, dtype=jnp.int32)[:, None] * t + jnp.arange(t, dtype=jnp.int32)[None, :])
    spare = TOP_K * t + jnp.arange((n_grid + 1) * tm, dtype=jnp.int32) % tm
    dst_all = spare.at[(tm + pos).reshape(-1)].set(slot.reshape(-1), unique_indices=True,
                                                   mode="promise_in_bounds")
    tile_start = jnp.arange(n_grid, dtype=jnp.int32) * tm
    tile_expert = jnp.minimum(jnp.sum(ends[None, :] <= tile_start[:, None], axis=1), n_e - 1).astype(jnp.int32)
    n_used = (ends[-1] // tm).astype(jnp.int32).reshape(1)
    ids = jnp.arange(n_e, dtype=jnp.int32)
    later_used = jnp.logical_and(ids[None, :] > ids[:, None], padded[None, :] > 0)
    nxt = jnp.min(jnp.where(later_used, ids[None, :], n_e), axis=1)
    nxt = jnp.where(nxt == n_e, -1, nxt)
    next_expert = jnp.sum(jnp.where(tile_expert[:, None] == ids[None, :], nxt[None, :], 0), axis=1)
    return dst_all, tile_expert, n_used, next_expert.astype(jnp.int32)


def kernel(x, p, w_in, w_pool, pool_scale, gmlp_ln_g, gmlp_ln_b, w_spatial, b_spatial, w_conv, norm_g_gmlp,
           norm_g_conv, w_out, ln1_g, ln1_b, w_router, router_bias, w_exp_gate, w_exp_up, w_exp_down,
           w_sh_gate, w_sh_up, w_sh_down, ln2_g, ln2_b, w_ple_gate, b_ple_gate, w_ple_proj, ln3_g, ln3_b):
    depth = w_in.shape[0]
    batch, seq, d = x.shape
    t = batch * seq
    n_e = w_router.shape[2]
    alpha = (2.0 * depth) ** 0.25
    row = lambda a: a.reshape(1, -1)

    tm_e = _tile(t * TOP_K // n_e, 256, V7X_SUBLANES * 2)
    n_grid = t * TOP_K // tm_e + n_e

    sgu16 = jnp.concatenate([w_sh_gate, w_sh_up], axis=-1).astype(BF16)
    sd16 = w_sh_down.astype(BF16)

    h32 = x.reshape(t, d)
    h16 = h32.astype(BF16)
    p2 = p.reshape(depth, t, p.shape[-1])
    for i in range(depth):
        z = matmul(h16, w_in[i].astype(BF16), BF16)
        ycat = mixer(z, batch, seq, w_pool[i].astype(BF16), row(pool_scale[i]), row(gmlp_ln_g[i]),
                     row(gmlp_ln_b[i]), w_spatial[i], b_spatial[i].T, w_conv[i], row(norm_g_gmlp[i]),
                     row(norm_g_conv[i]))
        h32, = proj_ln(ycat, w_out[i].astype(BF16), h32, row(ln1_g[i]), row(ln1_b[i]), alpha)
        wr_t = w_router[i].T
        wr_hi = wr_t.astype(BF16)
        wr_lo = (wr_t - wr_hi.astype(F32)).astype(BF16)
        eidx_t, wts_t, rank_t, counts = router(h32, wr_hi, wr_lo, router_bias[i].reshape(n_e, 1))
        dst_all, tile_expert, n_used, next_expert = _dispatch_plan(eidx_t, rank_t, counts[:, 0], tm_e, n_grid)
        y_slots = moe_grouped(h32, dst_all, tile_expert, n_used, next_expert, w_exp_gate, w_exp_up, w_exp_down,
                              i, tm_e, TOP_K * t + tm_e)
        y_shared = shared_swiglu(h32, sgu16, sd16, i)
        h32, h16 = combine_ln(h32, y_shared, y_slots, wts_t.T, row(ln2_g[i]), row(ln2_b[i]), alpha)
        h32, h16 = proj_ln(h16, w_ple_gate[i].astype(BF16), h32, row(ln3_g[i]), row(ln3_b[i]), alpha,
                           ple=(row(b_ple_gate[i]), p2[i], w_ple_proj[i].astype(BF16)))
    return h32.reshape(batch, seq, d)
```
